```python
import math
import jax, jax.numpy as jnp
from jax import lax
import numpy as np

D_MODEL = 2048
BATCH = 4
SEQ = 4096
DEPTH = 1

S5_WIDTH = D_MODEL // 2
S5_GROUP = 16
S5_GROUPS = S5_WIDTH // S5_GROUP
S5_STATE = 64
DT_MIN = 1e-3
DT_MAX = 1e-1
RET_HEADS = 4
RET_QK_DIM = D_MODEL // 8
RET_V_DIM = 2 * RET_QK_DIM
RET_QK_WIDTH = RET_HEADS * RET_QK_DIM
RET_V_WIDTH = RET_HEADS * RET_V_DIM
RET_CHUNK = 128
ROPE_BASE = 10000.0
MEM_LEN = 256
XATTN_HEADS = 4
XATTN_HEAD_DIM = D_MODEL // XATTN_HEADS
D_FF = 256 * ((8 * D_MODEL // 3 + 255) // 256)
N_IN = S5_WIDTH + 2 * RET_QK_WIDTH + 2 * RET_V_WIDTH + 2 * D_MODEL
RMS_EPS = 1e-6
GN_EPS = 1e-5

kernel_name = "hybrid_s5_retention_gated_macaron"


def rms_norm(x, g):
    xf = x.astype(jnp.float32)
    y = xf * lax.rsqrt(jnp.mean(xf * xf, axis=-1, keepdims=True) + RMS_EPS)
    return (y * g.astype(jnp.float32)).astype(x.dtype)


def swiglu(h, w1, w3, w2):
    return (jax.nn.silu(h @ w1) * (h @ w3)) @ w2


def _complex_affine_combine(e1, e2):
    a1r, a1i, b1r, b1i = e1
    a2r, a2i, b2r, b2i = e2
    return (a2r * a1r - a2i * a1i,
            a2r * a1i + a2i * a1r,
            a2r * b1r - a2i * b1i + b2r,
            a2r * b1i + a2i * b1r + b2i)


def s5_branch(u, a_re, a_im, log_dt, b_re, b_im, c_re, c_im, d_skip, w_v, w_g):
    f32 = jnp.float32
    bsz, seq, _ = u.shape
    uf = u.astype(f32)
    ug = uf.reshape(bsz, seq, S5_GROUPS, S5_GROUP)
    dt = jnp.exp(log_dt.astype(f32))[:, None]
    ar = a_re.astype(f32)
    ai = a_im.astype(f32)
    mag = jnp.exp(ar * dt)
    lbar_re = mag * jnp.cos(ai * dt)
    lbar_im = mag * jnp.sin(ai * dt)
    den = ar * ar + ai * ai
    nr = lbar_re - 1.0
    ni = lbar_im
    f_re = (nr * ar + ni * ai) / den
    f_im = (ni * ar - nr * ai) / den
    br = b_re.astype(f32)
    bi = b_im.astype(f32)
    bbar_re = f_re[..., None] * br - f_im[..., None] * bi
    bbar_im = f_re[..., None] * bi + f_im[..., None] * br
    bu_re = jnp.einsum('blgc,gpc->blgp', ug, bbar_re)
    bu_im = jnp.einsum('blgc,gpc->blgp', ug, bbar_im)
    a_t_re = jnp.broadcast_to(lbar_re, bu_re.shape)
    a_t_im = jnp.broadcast_to(lbar_im, bu_im.shape)
    _, _, s_re, s_im = lax.associative_scan(
        _complex_affine_combine, (a_t_re, a_t_im, bu_re, bu_im), axis=1)
    y = (jnp.einsum('blgp,gcp->blgc', s_re, c_re.astype(f32))
         - jnp.einsum('blgp,gcp->blgc', s_im, c_im.astype(f32)))
    y = y.reshape(bsz, seq, S5_WIDTH) + d_skip.astype(f32) * uf
    y = jax.nn.gelu(y).astype(u.dtype)
    return (y @ w_v) * jax.nn.sigmoid(y @ w_g)


def rotary(t):
    seq, d = t.shape[1], t.shape[-1]
    inv = ROPE_BASE ** (-jnp.arange(0, d, 2, dtype=jnp.float32) / d)
    ang = jnp.arange(seq, dtype=jnp.float32)[:, None] * inv[None, :]
    cos = jnp.cos(ang)[None, :, None, :]
    sin = jnp.sin(ang)[None, :, None, :]
    t1, t2 = t[..., : d // 2], t[..., d // 2:]
    return jnp.concatenate([t1 * cos - t2 * sin, t1 * sin + t2 * cos], axis=-1)


def retention_branch(q, k, v, g, w_o):
    f32 = jnp.float32
    bsz, seq, _ = q.shape
    n_chunks = seq // RET_CHUNK
    q = rotary(q.astype(f32).reshape(bsz, seq, RET_HEADS, RET_QK_DIM))
    k = rotary(k.astype(f32).reshape(bsz, seq, RET_HEADS, RET_QK_DIM)) * (RET_QK_DIM ** -0.5)
    v = v.astype(f32).reshape(bsz, seq, RET_HEADS, RET_V_DIM)

    def to_chunks(t):
        return t.reshape(bsz, n_chunks, RET_CHUNK, RET_HEADS, t.shape[-1]).transpose(1, 0, 3, 2, 4)

    log_gamma = jnp.log(1.0 - 2.0 ** (-5.0 - jnp.arange(RET_HEADS, dtype=f32)))
    idx = jnp.arange(RET_CHUNK, dtype=f32)
    rel = idx[:, None] - idx[None, :]
    decay_inner = jnp.where(rel[None] >= 0,
                            jnp.exp(jnp.maximum(rel, 0.0)[None] * log_gamma[:, None, None]), 0.0)
    xi = jnp.exp((idx + 1.0)[None, :] * log_gamma[:, None])[None, :, :, None]
    zeta = jnp.exp((RET_CHUNK - 1.0 - idx)[None, :] * log_gamma[:, None])[None, :, :, None]
    gamma_c = jnp.exp(RET_CHUNK * log_gamma)[None, :, None, None]

    def step(state, qkv):
        qc, kc, vc = qkv
        scores = jnp.einsum('bhid,bhjd->bhij', qc, kc) * decay_inner
        out = (jnp.einsum('bhij,bhje->bhie', scores, vc)
               + jnp.einsum('bhid,bhde->bhie', qc, state) * xi)
        state = state * gamma_c + jnp.einsum('bhjd,bhje->bhde', kc * zeta, vc)
        return state, out

    state0 = jnp.zeros((bsz, RET_HEADS, RET_QK_DIM, RET_V_DIM), f32)
    _, o = lax.scan(step, state0, (to_chunks(q), to_chunks(k), to_chunks(v)))
    o = o.transpose(1, 0, 3, 2, 4).reshape(bsz, seq, RET_HEADS, RET_V_DIM)
    mean = jnp.mean(o, axis=-1, keepdims=True)
    var = jnp.mean(jnp.square(o - mean), axis=-1, keepdims=True)
    o = ((o - mean) * lax.rsqrt(var + GN_EPS)).reshape(bsz, seq, RET_V_WIDTH)
    o = (jax.nn.silu(g.astype(f32)) * o).astype(g.dtype)
    return o @ w_o


def cross_attention(h, m, wq, wk, wv, wo):
    bsz, seq, _ = h.shape
    mlen = m.shape[1]
    q = (h @ wq).reshape(bsz, seq, XATTN_HEADS, XATTN_HEAD_DIM).astype(jnp.float32)
    k = (m @ wk).reshape(bsz, mlen, XATTN_HEADS, XATTN_HEAD_DIM).astype(jnp.float32)
    v = (m @ wv).reshape(bsz, mlen, XATTN_HEADS, XATTN_HEAD_DIM).astype(jnp.float32)
    s = jnp.einsum('blhd,bmhd->bhlm', q, k) * (XATTN_HEAD_DIM ** -0.5)
    p = jax.nn.softmax(s, axis=-1)
    o = jnp.einsum('bhlm,bmhd->blhd', p, v).reshape(bsz, seq, D_MODEL).astype(h.dtype)
    return o @ wo


def setup_inputs(seed: int = 0) -> dict:
    key = jax.random.key(seed)
    ks = iter(jax.random.split(key, 40))
    f32 = jnp.float32

    def dense(shape, fan_in):
        return jax.random.normal(next(ks), shape, f32) * (fan_in ** -0.5)

    def gain(shape):
        return 1.0 + 0.02 * jax.random.normal(next(ks), shape, f32)

    L = DEPTH
    n_idx = jnp.arange(S5_STATE, dtype=f32)
    a_re = -0.5 + 0.01 * jax.random.normal(next(ks), (L, S5_GROUPS, S5_STATE), f32)
    a_im = jnp.pi * n_idx[None, None, :] + 0.01 * jax.random.normal(next(ks), (L, S5_GROUPS, S5_STATE), f32)
    log_dt = jax.random.uniform(next(ks), (L, S5_GROUPS), f32, math.log(DT_MIN), math.log(DT_MAX))
    return {
        "x": jax.random.normal(next(ks), (BATCH, SEQ, D_MODEL), f32),
        "mem": jax.random.normal(next(ks), (BATCH, MEM_LEN, D_MODEL), f32),
        "ffn1_norm": gain((L, D_MODEL)),
        "ffn1_w1": dense((L, D_MODEL, D_FF), D_MODEL),
        "ffn1_w3": dense((L, D_MODEL, D_FF), D_MODEL),
        "ffn1_w2": dense((L, D_FF, D_MODEL), D_FF),
        "mix_norm": gain((L, D_MODEL)),
        "w_in": dense((L, D_MODEL, N_IN), D_MODEL),
        "s5_a_re": a_re,
        "s5_a_im": a_im,
        "s5_log_dt": log_dt,
        "s5_b_re": dense((L, S5_GROUPS, S5_STATE, S5_GROUP), 2 * S5_GROUP),
        "s5_b_im": dense((L, S5_GROUPS, S5_STATE, S5_GROUP), 2 * S5_GROUP),
        "s5_c_re": dense((L, S5_GROUPS, S5_GROUP, S5_STATE), S5_STATE / 4.0),
        "s5_c_im": dense((L, S5_GROUPS, S5_GROUP, S5_STATE), S5_STATE / 4.0),
        "s5_d": jax.random.normal(next(ks), (L, S5_WIDTH), f32),
        "s5_glu_v": dense((L, S5_WIDTH, D_MODEL), S5_WIDTH),
        "s5_glu_g": dense((L, S5_WIDTH, D_MODEL), S5_WIDTH),
        "ret_w_o": dense((L, RET_V_WIDTH, D_MODEL), RET_V_WIDTH),
        "w_out": dense((L, D_MODEL, D_MODEL), D_MODEL),
        "xattn_norm": gain((L, D_MODEL)),
        "mem_norm": gain((L, D_MODEL)),
        "xattn_wq": dense((L, D_MODEL, D_MODEL), D_MODEL),
        "xattn_wk": dense((L, D_MODEL, D_MODEL), D_MODEL),
        "xattn_wv": dense((L, D_MODEL, D_MODEL), D_MODEL),
        "xattn_wo": dense((L, D_MODEL, D_MODEL), D_MODEL),
        "ffn2_norm": gain((L, D_MODEL)),
        "ffn2_w1": dense((L, D_MODEL, D_FF), D_MODEL),
        "ffn2_w3": dense((L, D_MODEL, D_FF), D_MODEL),
        "ffn2_w2": dense((L, D_FF, D_MODEL), D_FF),
        "final_norm": gain((D_MODEL,)),
    }


def reference(x, mem, ffn1_norm, ffn1_w1, ffn1_w3, ffn1_w2, mix_norm, w_in,
              s5_a_re, s5_a_im, s5_log_dt, s5_b_re, s5_b_im, s5_c_re, s5_c_im, s5_d,
              s5_glu_v, s5_glu_g, ret_w_o, w_out, xattn_norm, mem_norm,
              xattn_wq, xattn_wk, xattn_wv, xattn_wo,
              ffn2_norm, ffn2_w1, ffn2_w3, ffn2_w2, final_norm):
    splits = np.cumsum([S5_WIDTH, RET_QK_WIDTH, RET_QK_WIDTH, RET_V_WIDTH, RET_V_WIDTH, D_MODEL])
    h = x
    for l in range(DEPTH):
        h = h + 0.5 * swiglu(rms_norm(h, ffn1_norm[l]), ffn1_w1[l], ffn1_w3[l], ffn1_w2[l])
        u = rms_norm(h, mix_norm[l])
        proj = u @ w_in[l]
        u_s5, q, k, v, g_ret, gate_a, gate_b = jnp.split(proj, splits, axis=-1)
        y_a = s5_branch(u_s5, s5_a_re[l], s5_a_im[l], s5_log_dt[l], s5_b_re[l], s5_b_im[l],
                        s5_c_re[l], s5_c_im[l], s5_d[l], s5_glu_v[l], s5_glu_g[l])
        y_b = retention_branch(q, k, v, g_ret, ret_w_o[l])
        merged = jax.nn.sigmoid(gate_a) * y_a + jax.nn.sigmoid(gate_b) * y_b
        h = h + merged @ w_out[l]
        h = h + cross_attention(rms_norm(h, xattn_norm[l]), rms_norm(mem, mem_norm[l]),
                                xattn_wq[l], xattn_wk[l], xattn_wv[l], xattn_wo[l])
        h = h + 0.5 * swiglu(rms_norm(h, ffn2_norm[l]), ffn2_w1[l], ffn2_w3[l], ffn2_w2[l])
    return rms_norm(h, final_norm)
```

```python
import functools

import jax
import jax.numpy as jnp
from jax import lax
from jax.experimental import pallas as pl
from jax.experimental.pallas import tpu as pltpu

F32 = jnp.float32
BF16 = jnp.bfloat16

RMS_EPS = 1e-6
GN_EPS = 1e-5
ROPE_BASE = 10000.0
S5_GROUP = 16
RET_HEADS = 4
XATTN_HEADS = 4
RET_CHUNK = 128
S5_CHUNK = 16
LANES = 128
GROUPS_PER_LANE_BLOCK = LANES // S5_GROUP
VMEM_LIMIT_BYTES = 56 * 1024 * 1024


def _params(semantics):
    return pltpu.CompilerParams(dimension_semantics=semantics,
                                vmem_limit_bytes=VMEM_LIMIT_BYTES)


def _rms(x, gain):
    ms = jnp.mean(x * x, axis=-1, keepdims=True)
    return x * lax.rsqrt(ms + RMS_EPS) * gain


def _dot(a, b):
    return jnp.dot(a, b, preferred_element_type=F32)


def _dot_nt(a, b):
    return lax.dot_general(a, b, (((1,), (1,)), ((), ())), preferred_element_type=F32)


def _dot_tn(a, b):
    return lax.dot_general(a, b, (((0,), (0,)), ((), ())), preferred_element_type=F32)


def _ffn_kernel(*refs, final):
    if final:
        x_ref, g_ref, w1_ref, w3_ref, w2_ref, fg_ref, o_ref, xn_ref, acc_ref = refs
    else:
        x_ref, g_ref, w1_ref, w3_ref, w2_ref, o_ref, xn_ref, acc_ref = refs
    j = pl.program_id(1)

    @pl.when(j == 0)
    def _():
        xn_ref[...] = _rms(x_ref[...], g_ref[...]).astype(BF16)
        acc_ref[...] = jnp.zeros_like(acc_ref)

    xn = xn_ref[...]
    a = _dot(xn, w1_ref[...])
    b = _dot(xn, w3_ref[...])
    mid = (a * jax.nn.sigmoid(a) * b).astype(BF16)
    acc_ref[...] += _dot(mid, w2_ref[...])

    @pl.when(j == pl.num_programs(1) - 1)
    def _():
        h = x_ref[...] + 0.5 * acc_ref[...]
        if final:
            h = _rms(h, fg_ref[...])
        o_ref[...] = h


def _ffn(x, gain, w1, w3, w2, final_gain=None, *, tm=512, tf=512):
    t, d = x.shape
    f = w1.shape[1]
    tm, tf = min(tm, t), min(tf, f)
    final = final_gain is not None
    in_specs = [
        pl.BlockSpec((tm, d), lambda i, j: (i, 0)),
        pl.BlockSpec((1, d), lambda i, j: (0, 0)),
        pl.BlockSpec((d, tf), lambda i, j: (0, j)),
        pl.BlockSpec((d, tf), lambda i, j: (0, j)),
        pl.BlockSpec((tf, d), lambda i, j: (j, 0)),
    ]
    args = [x, gain.reshape(1, d), w1, w3, w2]
    if final:
        in_specs.append(pl.BlockSpec((1, d), lambda i, j: (0, 0)))
        args.append(final_gain.reshape(1, d))
    return pl.pallas_call(
        functools.partial(_ffn_kernel, final=final),
        out_shape=jax.ShapeDtypeStruct((t, d), F32),
        grid=(t // tm, f // tf),
        in_specs=in_specs,
        out_specs=pl.BlockSpec((tm, d), lambda i, j: (i, 0)),
        scratch_shapes=[pltpu.VMEM((tm, d), BF16), pltpu.VMEM((tm, d), F32)],
        compiler_params=_params(("parallel", "arbitrary")),
        name="ffn_final" if final else "ffn",
    )(*args)


def _inproj_kernel(x_ref, g_ref, w_ref, proj_ref, us5_ref, xn_ref):
    j = pl.program_id(1)

    @pl.when(j == 0)
    def _():
        xn_ref[...] = _rms(x_ref[...], g_ref[...]).astype(BF16)

    res = _dot(xn_ref[...], w_ref[...]).astype(BF16)

    @pl.when(j == 0)
    def _():
        for k in range(us5_ref.shape[0]):
            us5_ref[k] = res[:, k * LANES:(k + 1) * LANES]

    @pl.when(j > 0)
    def _():
        proj_ref[...] = res


def _inproj(x, gain, w, s5_width, *, tm=512):
    t, d = x.shape
    n = w.shape[1]
    tn = s5_width
    tm = min(tm, t)
    nlb = s5_width // LANES
    return pl.pallas_call(
        _inproj_kernel,
        out_shape=(jax.ShapeDtypeStruct((t, n - tn), BF16),
                   jax.ShapeDtypeStruct((nlb, t, LANES), BF16)),
        grid=(t // tm, n // tn),
        in_specs=[
            pl.BlockSpec((tm, d), lambda i, j: (i, 0)),
            pl.BlockSpec((1, d), lambda i, j: (0, 0)),
            pl.BlockSpec((d, tn), lambda i, j: (0, j)),
        ],
        out_specs=(
            pl.BlockSpec((tm, tn), lambda i, j: (i, jnp.maximum(j - 1, 0))),
            pl.BlockSpec((nlb, tm, LANES), lambda i, j: (0, i, 0)),
        ),
        scratch_shapes=[pltpu.VMEM((tm, d), BF16)],
        compiler_params=_params(("parallel", "arbitrary")),
        name="inproj",
    )(x, gain.reshape(1, d), w)


def _normmm_kernel(x_ref, g_ref, w_ref, o_ref):
    xn = _rms(x_ref[...], g_ref[...]).astype(BF16)
    o_ref[...] = _dot(xn, w_ref[...]).astype(BF16)


def _normmm(x, gain, w, *, tn=512):
    t, d = x.shape
    n = w.shape[1]
    tn = min(tn, n)
    return pl.pallas_call(
        _normmm_kernel,
        out_shape=jax.ShapeDtypeStruct((t, n), BF16),
        grid=(n // tn,),
        in_specs=[
            pl.BlockSpec((t, d), lambda j: (0, 0)),
            pl.BlockSpec((1, d), lambda j: (0, 0)),
            pl.BlockSpec((d, tn), lambda j: (0, j)),
        ],
        out_specs=pl.BlockSpec((t, tn), lambda j: (0, j)),
        compiler_params=_params(("parallel",)),
        name="normmm",
    )(x, gain.reshape(1, d), w)


def _s5_kernel(u_ref, m_ref, win_ref, wout_ref, lam_ref, d_ref, y_ref, z_ref, s_ref):
    u = u_ref[0]
    z_ref[...] = _dot(u, win_ref[0])
    ns = z_ref.shape[1] // 2
    lam_re = lam_ref[0, :, :ns]
    lam_im = lam_ref[0, :, ns:]

    def step(c, carry):
        s_re, s_im = carry
        s_ref[pl.ds(c, 1), :ns] = s_re
        s_ref[pl.ds(c, 1), ns:] = s_im
        z_re = z_ref[pl.ds(c, 1), :ns]
        z_im = z_ref[pl.ds(c, 1), ns:]
        return (lam_re * s_re - lam_im * s_im + z_re,
                lam_re * s_im + lam_im * s_re + z_im)

    zero = jnp.zeros((1, ns), F32)
    lax.fori_loop(0, z_ref.shape[0], step, (zero, zero))

    y = (_dot(u, m_ref[0]) + _dot(s_ref[...].astype(BF16), wout_ref[0])
         + d_ref[0] * u.astype(F32))
    y_ref[0] = jax.nn.gelu(y).astype(BF16)


def _s5(u, m, w_in, w_out, lam, dtile, batch):
    nlb, rows, width = u.shape
    rb = rows // batch
    nstate = w_in.shape[2]
    return pl.pallas_call(
        _s5_kernel,
        out_shape=jax.ShapeDtypeStruct((nlb, rows, width), BF16),
        grid=(nlb, batch),
        in_specs=[
            pl.BlockSpec((1, rb, width), lambda k, b: (k, b, 0)),
            pl.BlockSpec((1, width, width), lambda k, b: (k, 0, 0)),
            pl.BlockSpec((1, width, nstate), lambda k, b: (k, 0, 0)),
            pl.BlockSpec((1, nstate, width), lambda k, b: (k, 0, 0)),
            pl.BlockSpec((1, 1, nstate), lambda k, b: (k, 0, 0)),
            pl.BlockSpec((1, 1, width), lambda k, b: (k, 0, 0)),
        ],
        out_specs=pl.BlockSpec((1, rb, width), lambda k, b: (k, b, 0)),
        scratch_shapes=[pltpu.VMEM((rb, nstate), F32), pltpu.VMEM((rb, nstate), F32)],
        compiler_params=_params(("parallel", "arbitrary")),
        name="s5",
    )(u, m, w_in, w_out, lam, dtile)


def _s5_weights(a_re, a_im, log_dt, b_re, b_im, c_re, c_im, d_skip):
    hi = lax.Precision.HIGHEST
    tc = S5_CHUNK
    g, p = a_re.shape
    gl = GROUPS_PER_LANE_BLOCK
    nlb = g // gl
    dt = jnp.exp(log_dt)[:, None]
    mag = jnp.exp(a_re * dt)
    l_re = mag * jnp.cos(a_im * dt)
    l_im = mag * jnp.sin(a_im * dt)
    den = a_re * a_re + a_im * a_im
    n_re = l_re - 1.0
    n_im = l_im
    f_re = (n_re * a_re + n_im * a_im) / den
    f_im = (n_im * a_re - n_re * a_im) / den
    bb_re = f_re[..., None] * b_re - f_im[..., None] * b_im
    bb_im = f_re[..., None] * b_im + f_im[..., None] * b_re
    p_re, p_im = [jnp.ones_like(l_re)], [jnp.zeros_like(l_im)]
    for _ in range(tc):
        p_re, p_im = (p_re + [p_re[-1] * l_re - p_im[-1] * l_im],
                      p_im + [p_re[-1] * l_im + p_im[-1] * l_re])
    p_re = jnp.stack(p_re)
    p_im = jnp.stack(p_im)
    lb_re = p_re[:tc, :, :, None] * bb_re[None] - p_im[:tc, :, :, None] * bb_im[None]
    lb_im = p_re[:tc, :, :, None] * bb_im[None] + p_im[:tc, :, :, None] * bb_re[None]
    kern = (jnp.einsum('gop,tgpi->tgio', c_re, lb_re, precision=hi)
            - jnp.einsum('gop,tgpi->tgio', c_im, lb_im, precision=hi))
    eye = jnp.eye(gl, dtype=F32)
    kpad = jnp.concatenate([jnp.zeros((tc - 1,) + kern.shape[1:], F32), kern], axis=0)
    idx = jnp.arange(tc)[None, :] - jnp.arange(tc)[:, None] + (tc - 1)
    m6 = kpad[idx].reshape(tc, tc, nlb, gl, S5_GROUP, S5_GROUP)
    m6 = m6.transpose(2, 0, 3, 4, 1, 5)
    toep = (m6[:, :, :, :, :, None, :] * eye[None, None, :, None, None, :, None])
    toep = toep.reshape(nlb, tc * LANES, tc * LANES).astype(BF16)
    lb = jnp.stack([lb_re[::-1], lb_im[::-1]])
    lb = lb.reshape(2, tc, nlb, gl, p, S5_GROUP).transpose(2, 1, 3, 5, 0, 4)
    w_in = lb[:, :, :, :, :, None, :] * eye[None, None, :, None, None, :, None]
    w_in = w_in.reshape(nlb, tc * LANES, 2 * gl * p).astype(BF16)
    q_re = p_re[1:, :, None, :]
    q_im = p_im[1:, :, None, :]
    wo_re = c_re[None] * q_re - c_im[None] * q_im
    wo_im = -(c_re[None] * q_im + c_im[None] * q_re)
    wo = jnp.stack([wo_re, wo_im])
    wo = wo.reshape(2, tc, nlb, gl, S5_GROUP, p).transpose(2, 0, 3, 5, 1, 4)
    w_out = wo[:, :, :, :, :, None, :] * eye[None, None, :, None, None, :, None]
    w_out = w_out.reshape(nlb, 2 * gl * p, tc * LANES).astype(BF16)
    lam = jnp.stack([p_re[tc], p_im[tc]]).reshape(2, nlb, gl * p)
    lam = lam.transpose(1, 0, 2).reshape(nlb, 1, 2 * gl * p)
    dtile = jnp.tile(d_skip.reshape(nlb, 1, LANES), (1, 1, tc))
    return toep, w_in, w_out, lam, dtile


def _ret_kernel(q_ref, k_ref, v_ref, g_ref, cos_ref, sin_ref, dec_ref, xi_ref,
                zeta_ref, gam_ref, o_ref, state_ref, *, heads):
    c = pl.program_id(1)

    @pl.when(c == 0)
    def _():
        state_ref[...] = jnp.zeros_like(state_ref)

    dqk = q_ref.shape[1] // heads
    dv = v_ref.shape[1] // heads
    half = dqk // 2
    cos = cos_ref[...]
    sin = sin_ref[...]

    def rot(t):
        t1, t2 = t[:, :half], t[:, half:]
        return jnp.concatenate([t1 * cos - t2 * sin, t1 * sin + t2 * cos], axis=-1)

    for h in range(heads):
        q = rot(q_ref[:, h * dqk:(h + 1) * dqk].astype(F32))
        k = rot(k_ref[:, h * dqk:(h + 1) * dqk].astype(F32)) * (dqk ** -0.5)
        v = v_ref[:, h * dv:(h + 1) * dv]
        qb = q.astype(BF16)
        scores = _dot_nt(qb, k.astype(BF16)) * dec_ref[h]
        st = state_ref[h]
        out = _dot(scores.astype(BF16), v) + _dot(qb, st.astype(BF16)) * xi_ref[h]
        kz = (k * zeta_ref[h]).astype(BF16)
        state_ref[h] = st * gam_ref[h] + _dot_tn(kz, v)
        mean = jnp.mean(out, axis=-1, keepdims=True)
        cen = out - mean
        var = jnp.mean(cen * cen, axis=-1, keepdims=True)
        normed = cen * lax.rsqrt(var + GN_EPS)
        g = g_ref[:, h * dv:(h + 1) * dv].astype(F32)
        o_ref[:, h * dv:(h + 1) * dv] = (g * jax.nn.sigmoid(g) * normed).astype(BF16)


def _retention(proj, d, batch, seq):
    t = proj.shape[0]
    heads = RET_HEADS
    ch = min(RET_CHUNK, seq)
    nc = seq // ch
    dqk = d // 2 // heads
    dv = d // heads
    half = dqk // 2
    inv = ROPE_BASE ** (-jnp.arange(0, dqk, 2, dtype=F32) / dqk)
    ang = jnp.arange(seq, dtype=F32)[:, None] * inv[None, :]
    cos, sin = jnp.cos(ang), jnp.sin(ang)
    log_gamma = jnp.log(1.0 - 2.0 ** (-5.0 - jnp.arange(heads, dtype=F32)))
    idx = jnp.arange(ch, dtype=F32)
    rel = idx[:, None] - idx[None, :]
    dec = jnp.where(rel[None] >= 0,
                    jnp.exp(jnp.maximum(rel, 0.0)[None] * log_gamma[:, None, None]), 0.0)
    xi = jnp.exp((idx + 1.0)[None, :] * log_gamma[:, None])[:, :, None]
    zeta = jnp.exp((ch - 1.0 - idx)[None, :] * log_gamma[:, None])[:, :, None]
    gam = jnp.exp(ch * log_gamma)[:, None, None]
    row = lambda b, c: b * nc + c
    return pl.pallas_call(
        functools.partial(_ret_kernel, heads=heads),
        out_shape=jax.ShapeDtypeStruct((t, d), BF16),
        grid=(batch, nc),
        in_specs=[
            pl.BlockSpec((ch, d // 2), lambda b, c: (row(b, c), 0)),
            pl.BlockSpec((ch, d // 2), lambda b, c: (row(b, c), 1)),
            pl.BlockSpec((ch, d), lambda b, c: (row(b, c), 1)),
            pl.BlockSpec((ch, d), lambda b, c: (row(b, c), 2)),
            pl.BlockSpec((ch, half), lambda b, c: (c, 0)),
            pl.BlockSpec((ch, half), lambda b, c: (c, 0)),
            pl.BlockSpec((heads, ch, ch), lambda b, c: (0, 0, 0)),
            pl.BlockSpec((heads, ch, 1), lambda b, c: (0, 0, 0)),
            pl.BlockSpec((heads, ch, 1), lambda b, c: (0, 0, 0)),
            pl.BlockSpec((heads, 1, 1), lambda b, c: (0, 0, 0)),
        ],
        out_specs=pl.BlockSpec((ch, d), lambda b, c: (row(b, c), 0)),
        scratch_shapes=[pltpu.VMEM((heads, dqk, dv), F32)],
        compiler_params=_params(("parallel", "arbitrary")),
        name="retention",
    )(proj, proj, proj, proj, cos, sin, dec, xi, zeta, gam)


def _mix_kernel(og_ref, yg_ref, ga_ref, gb_ref, h_ref, wo_ref, wv_ref, wg_ref,
                wout_ref, o_ref, ygc_ref, acc_ref):
    j = pl.program_id(1)

    @pl.when(j == 0)
    def _():
        for k in range(yg_ref.shape[0]):
            ygc_ref[:, k * LANES:(k + 1) * LANES] = yg_ref[k]
        acc_ref[...] = jnp.zeros_like(acc_ref)

    yg = ygc_ref[...]
    y_b = _dot(og_ref[...], wo_ref[...])
    y_a = _dot(yg, wv_ref[...]) * jax.nn.sigmoid(_dot(yg, wg_ref[...]))
    merged = (jax.nn.sigmoid(ga_ref[...].astype(F32)) * y_a
              + jax.nn.sigmoid(gb_ref[...].astype(F32)) * y_b)
    acc_ref[...] += _dot(merged.astype(BF16), wout_ref[...])

    @pl.when(j == pl.num_programs(1) - 1)
    def _():
        o_ref[...] = h_ref[...] + acc_ref[...]


def _mix(og, yg, proj, h, w_o, w_v, w_g, w_out, *, tm=512, tn=512):
    t, d = h.shape
    nlb = yg.shape[0]
    sw = nlb * LANES
    tm, tn = min(tm, t), min(tn, d)
    ga0 = 3 * d // tn
    gb0 = 4 * d // tn
    return pl.pallas_call(
        _mix_kernel,
        out_shape=jax.ShapeDtypeStruct((t, d), F32),
        grid=(t // tm, d // tn),
        in_specs=[
            pl.BlockSpec((tm, d), lambda i, j: (i, 0)),
            pl.BlockSpec((nlb, tm, LANES), lambda i, j: (0, i, 0)),
            pl.BlockSpec((tm, tn), lambda i, j: (i, ga0 + j)),
            pl.BlockSpec((tm, tn), lambda i, j: (i, gb0 + j)),
            pl.BlockSpec((tm, d), lambda i, j: (i, 0)),
            pl.BlockSpec((d, tn), lambda i, j: (0, j)),
            pl.BlockSpec((sw, tn), lambda i, j: (0, j)),
            pl.BlockSpec((sw, tn), lambda i, j: (0, j)),
            pl.BlockSpec((tn, d), lambda i, j: (j, 0)),
        ],
        out_specs=pl.BlockSpec((tm, d), lambda i, j: (i, 0)),
        scratch_shapes=[pltpu.VMEM((tm, sw), BF16), pltpu.VMEM((tm, d), F32)],
        compiler_params=_params(("parallel", "arbitrary")),
        name="mix",
    )(og, yg, proj, proj, h, w_o, w_v, w_g, w_out)


def _xattn_kernel(h_ref, g_ref, wq_ref, k_ref, v_ref, wo_ref, o_ref, xn_ref, acc_ref):
    hd = pl.program_id(2)

    @pl.when(hd == 0)
    def _():
        xn_ref[...] = _rms(h_ref[...], g_ref[...]).astype(BF16)
        acc_ref[...] = jnp.zeros_like(acc_ref)

    dh = wq_ref.shape[1]
    q = _dot(xn_ref[...], wq_ref[...])
    s = _dot_nt(q.astype(BF16), k_ref[0]) * (dh ** -0.5)
    e = jnp.exp(s - jnp.max(s, axis=-1, keepdims=True))
    p = e / jnp.sum(e, axis=-1, keepdims=True)
    o = _dot(p.astype(BF16), v_ref[0])
    acc_ref[...] += _dot(o.astype(BF16), wo_ref[...])

    @pl.when(hd == pl.num_programs(2) - 1)
    def _():
        o_ref[...] = h_ref[...] + acc_ref[...]


def _xattn(h, gain, wq, kmem, vmem, wo, batch, *, tm=512):
    t, d = h.shape
    seq = t // batch
    mlen = kmem.shape[1]
    heads = XATTN_HEADS
    dh = d // heads
    tm = min(tm, seq)
    nt = seq // tm
    return pl.pallas_call(
        _xattn_kernel,
        out_shape=jax.ShapeDtypeStruct((t, d), F32),
        grid=(batch, nt, heads),
        in_specs=[
            pl.BlockSpec((tm, d), lambda b, i, hd: (b * nt + i, 0)),
            pl.BlockSpec((1, d), lambda b, i, hd: (0, 0)),
            pl.BlockSpec((d, dh), lambda b, i, hd: (0, hd)),
            pl.BlockSpec((1, mlen, dh), lambda b, i, hd: (b, 0, hd)),
            pl.BlockSpec((1, mlen, dh), lambda b, i, hd: (b, 0, hd)),
            pl.BlockSpec((dh, d), lambda b, i, hd: (hd, 0)),
        ],
        out_specs=pl.BlockSpec((tm, d), lambda b, i, hd: (b * nt + i, 0)),
        scratch_shapes=[pltpu.VMEM((tm, d), BF16), pltpu.VMEM((tm, d), F32)],
        compiler_params=_params(("parallel", "parallel", "arbitrary")),
        name="xattn",
    )(h, gain.reshape(1, d), wq, kmem, vmem, wo)


def kernel(x, mem, ffn1_norm, ffn1_w1, ffn1_w3, ffn1_w2, mix_norm, w_in, s5_a_re, s5_a_im, s5_log_dt, s5_b_re, s5_b_im, s5_c_re, s5_c_im, s5_d, s5_glu_v, s5_glu_g, ret_w_o, w_out, xattn_norm, mem_norm, xattn_wq, xattn_wk, xattn_wv, xattn_wo, ffn2_norm, ffn2_w1, ffn2_w3, ffn2_w2, final_norm):
    batch, seq, d = x.shape
    mlen = mem.shape[1]
    depth = ffn1_w1.shape[0]
    t = batch * seq
    s5_width = s5_d.shape[1]
    nlb = s5_width // LANES
    bf = lambda w: w.astype(BF16)

    h = x.reshape(t, d)
    mem2 = mem.reshape(batch * mlen, d)
    for l in range(depth):
        last = l == depth - 1
        h = _ffn(h, ffn1_norm[l], bf(ffn1_w1[l]), bf(ffn1_w3[l]), bf(ffn1_w2[l]))
        proj, us5 = _inproj(h, mix_norm[l], bf(w_in[l]), s5_width)
        toep, s_in, s_out, lam, dtile = _s5_weights(
            s5_a_re[l], s5_a_im[l], s5_log_dt[l], s5_b_re[l], s5_b_im[l],
            s5_c_re[l], s5_c_im[l], s5_d[l])
        u = us5.reshape(nlb, t // S5_CHUNK, S5_CHUNK * LANES)
        yg = _s5(u, toep, s_in, s_out, lam, dtile, batch).reshape(nlb, t, LANES)
        og = _retention(proj, d, batch, seq)
        h = _mix(og, yg, proj, h, bf(ret_w_o[l]), bf(s5_glu_v[l]), bf(s5_glu_g[l]),
                 bf(w_out[l]))
        kmem = _normmm(mem2, mem_norm[l], bf(xattn_wk[l])).reshape(batch, mlen, d)
        vmem = _normmm(mem2, mem_norm[l], bf(xattn_wv[l])).reshape(batch, mlen, d)
        h = _xattn(h, xattn_norm[l], bf(xattn_wq[l]), kmem, vmem, bf(xattn_wo[l]), batch)
        h = _ffn(h, ffn2_norm[l], bf(ffn2_w1[l]), bf(ffn2_w3[l]), bf(ffn2_w2[l]),
                 final_norm if last else None)
    if depth == 0:
        raise ValueError("depth must be >= 1")
    return h.reshape(batch, seq, d)
```

```python
import functools

import jax
import jax.numpy as jnp
from jax import lax
from jax.experimental import pallas as pl
from jax.experimental.pallas import tpu as pltpu

F32 = jnp.float32
BF16 = jnp.bfloat16

RMS_EPS = 1e-6
GN_EPS = 1e-5
ROPE_BASE = 10000.0
S5_GROUP = 16
RET_HEADS = 4
XATTN_HEADS = 4
RET_CHUNK = 128
S5_CHUNK = 16
LANES = 128
GROUPS_PER_LANE_BLOCK = LANES // S5_GROUP
VMEM_LIMIT_BYTES = 56 * 1024 * 1024


def _params(semantics):
    return pltpu.CompilerParams(dimension_semantics=semantics,
                                vmem_limit_bytes=VMEM_LIMIT_BYTES)


def _rms(x, gain):
    ms = jnp.mean(x * x, axis=-1, keepdims=True)
    return x * lax.rsqrt(ms + RMS_EPS) * gain


def _dot(a, b):
    return jnp.dot(a, b, preferred_element_type=F32)


def _dot_nt(a, b):
    return lax.dot_general(a, b, (((1,), (1,)), ((), ())), preferred_element_type=F32)


def _dot_tn(a, b):
    return lax.dot_general(a, b, (((0,), (0,)), ((), ())), preferred_element_type=F32)


def _ffn_kernel(*refs, final):
    if final:
        x_ref, g_ref, w1_ref, w3_ref, w2_ref, fg_ref, o_ref, xn_ref, acc_ref = refs
    else:
        x_ref, g_ref, w1_ref, w3_ref, w2_ref, o_ref, xn_ref, acc_ref = refs
    j = pl.program_id(1)

    @pl.when(j == 0)
    def _():
        xn_ref[...] = _rms(x_ref[...], g_ref[...]).astype(BF16)
        acc_ref[...] = jnp.zeros_like(acc_ref)

    xn = xn_ref[...]
    a = _dot(xn, w1_ref[...])
    b = _dot(xn, w3_ref[...])
    mid = (a * jax.nn.sigmoid(a) * b).astype(BF16)
    acc_ref[...] += _dot(mid, w2_ref[...])

    @pl.when(j == pl.num_programs(1) - 1)
    def _():
        h = x_ref[...] + 0.5 * acc_ref[...]
        if final:
            h = _rms(h, fg_ref[...])
        o_ref[...] = h


def _ffn(x, gain, w1, w3, w2, final_gain=None, *, tm=512, tf=512):
    t, d = x.shape
    f = w1.shape[1]
    tm, tf = min(tm, t), min(tf, f)
    final = final_gain is not None
    in_specs = [
        pl.BlockSpec((tm, d), lambda i, j: (i, 0)),
        pl.BlockSpec((1, d), lambda i, j: (0, 0)),
        pl.BlockSpec((d, tf), lambda i, j: (0, j)),
        pl.BlockSpec((d, tf), lambda i, j: (0, j)),
        pl.BlockSpec((tf, d), lambda i, j: (j, 0)),
    ]
    args = [x, gain.reshape(1, d), w1, w3, w2]
    if final:
        in_specs.append(pl.BlockSpec((1, d), lambda i, j: (0, 0)))
        args.append(final_gain.reshape(1, d))
    return pl.pallas_call(
        functools.partial(_ffn_kernel, final=final),
        out_shape=jax.ShapeDtypeStruct((t, d), F32),
        grid=(t // tm, f // tf),
        in_specs=in_specs,
        out_specs=pl.BlockSpec((tm, d), lambda i, j: (i, 0)),
        scratch_shapes=[pltpu.VMEM((tm, d), BF16), pltpu.VMEM((tm, d), F32)],
        compiler_params=_params(("parallel", "arbitrary")),
        name="ffn_final" if final else "ffn",
    )(*args)


def _inproj_kernel(x_ref, g_ref, w_ref, proj_ref, us5_ref, xn_ref):
    j = pl.program_id(1)

    @pl.when(j == 0)
    def _():
        xn_ref[...] = _rms(x_ref[...], g_ref[...]).astype(BF16)

    res = _dot(xn_ref[...], w_ref[...]).astype(BF16)

    @pl.when(j == 0)
    def _():
        for k in range(us5_ref.shape[0]):
            us5_ref[k] = res[:, k * LANES:(k + 1) * LANES]

    @pl.when(j > 0)
    def _():
        proj_ref[...] = res


def _inproj(x, gain, w, s5_width, *, tm=1024):
    t, d = x.shape
    n = w.shape[1]
    tn = s5_width
    tm = min(tm, t)
    nlb = s5_width // LANES
    return pl.pallas_call(
        _inproj_kernel,
        out_shape=(jax.ShapeDtypeStruct((t, n - tn), BF16),
                   jax.ShapeDtypeStruct((nlb, t, LANES), BF16)),
        grid=(t // tm, n // tn),
        in_specs=[
            pl.BlockSpec((tm, d), lambda i, j: (i, 0)),
            pl.BlockSpec((1, d), lambda i, j: (0, 0)),
            pl.BlockSpec((d, tn), lambda i, j: (0, j)),
        ],
        out_specs=(
            pl.BlockSpec((tm, tn), lambda i, j: (i, jnp.maximum(j - 1, 0))),
            pl.BlockSpec((nlb, tm, LANES), lambda i, j: (0, i, 0)),
        ),
        scratch_shapes=[pltpu.VMEM((tm, d), BF16)],
        compiler_params=_params(("parallel", "arbitrary")),
        name="inproj",
    )(x, gain.reshape(1, d), w)


def _normmm_kernel(x_ref, g_ref, w_ref, o_ref):
    xn = _rms(x_ref[...], g_ref[...]).astype(BF16)
    o_ref[...] = _dot(xn, w_ref[...]).astype(BF16)


def _normmm(x, gain, w, *, tn=512):
    t, d = x.shape
    n = w.shape[1]
    tn = min(tn, n)
    return pl.pallas_call(
        _normmm_kernel,
        out_shape=jax.ShapeDtypeStruct((t, n), BF16),
        grid=(n // tn,),
        in_specs=[
            pl.BlockSpec((t, d), lambda j: (0, 0)),
            pl.BlockSpec((1, d), lambda j: (0, 0)),
            pl.BlockSpec((d, tn), lambda j: (0, j)),
        ],
        out_specs=pl.BlockSpec((t, tn), lambda j: (0, j)),
        compiler_params=_params(("parallel",)),
        name="normmm",
    )(x, gain.reshape(1, d), w)


def _s5_kernel(u_ref, m_ref, win_ref, wout_ref, lam_ref, d_ref, y_ref, z_ref, s_ref):
    u = u_ref[0]
    z_ref[...] = _dot(u, win_ref[0])
    ns = z_ref.shape[1] // 2
    lam_re = lam_ref[0, :, :ns]
    lam_im = lam_ref[0, :, ns:]

    def step(c, carry):
        s_re, s_im = carry
        s_ref[pl.ds(c, 1), :ns] = s_re
        s_ref[pl.ds(c, 1), ns:] = s_im
        z_re = z_ref[pl.ds(c, 1), :ns]
        z_im = z_ref[pl.ds(c, 1), ns:]
        return (lam_re * s_re - lam_im * s_im + z_re,
                lam_re * s_im + lam_im * s_re + z_im)

    zero = jnp.zeros((1, ns), F32)
    lax.fori_loop(0, z_ref.shape[0], step, (zero, zero))

    y = (_dot(u, m_ref[0]) + _dot(s_ref[...].astype(BF16), wout_ref[0])
         + d_ref[0] * u.astype(F32))
    y_ref[0] = jax.nn.gelu(y).astype(BF16)


def _s5(u, m, w_in, w_out, lam, dtile, batch):
    nlb, rows, width = u.shape
    rb = rows // batch
    nstate = w_in.shape[2]
    return pl.pallas_call(
        _s5_kernel,
        out_shape=jax.ShapeDtypeStruct((nlb, rows, width), BF16),
        grid=(nlb, batch),
        in_specs=[
            pl.BlockSpec((1, rb, width), lambda k, b: (k, b, 0)),
            pl.BlockSpec((1, width, width), lambda k, b: (k, 0, 0)),
            pl.BlockSpec((1, width, nstate), lambda k, b: (k, 0, 0)),
            pl.BlockSpec((1, nstate, width), lambda k, b: (k, 0, 0)),
            pl.BlockSpec((1, 1, nstate), lambda k, b: (k, 0, 0)),
            pl.BlockSpec((1, 1, width), lambda k, b: (k, 0, 0)),
        ],
        out_specs=pl.BlockSpec((1, rb, width), lambda k, b: (k, b, 0)),
        scratch_shapes=[pltpu.VMEM((rb, nstate), F32), pltpu.VMEM((rb, nstate), F32)],
        compiler_params=_params(("parallel", "arbitrary")),
        name="s5",
    )(u, m, w_in, w_out, lam, dtile)


def _log2(n):
    assert n & (n - 1) == 0, n
    return n.bit_length() - 1


def _s5prep_kernel(bre_ref, bim_ref, cre_ref, cim_ref, lrow_ref, lcol_ref,
                   toep_ref, win_ref, wout_ref, lam_ref, *, tc):
    hi = lax.Precision.HIGHEST
    ns, gi = bre_ref.shape[1], bre_ref.shape[2]
    lanes, p = cre_ref.shape[1], cre_ref.shape[2]

    def iota(shape, dim):
        return lax.broadcasted_iota(jnp.int32, shape, dim)

    def grp(x, size):
        return lax.shift_right_logical(x, _log2(size))

    e_i = ((iota((gi, lanes), 1) & (gi - 1)) == iota((gi, lanes), 0)).astype(F32)
    e_p = ((iota((p, ns), 1) & (p - 1)) == iota((p, ns), 0)).astype(F32)
    mask_b = grp(iota((ns, lanes), 0), p) == grp(iota((ns, lanes), 1), gi)
    mask_c = grp(iota((lanes, ns), 0), gi) == grp(iota((lanes, ns), 1), p)

    def bdiag_b(ref):
        return jnp.where(mask_b, jnp.dot(ref[0], e_i, precision=hi,
                                         preferred_element_type=F32), 0.0).T

    def bdiag_c(ref):
        return jnp.where(mask_c, jnp.dot(ref[0], e_p, precision=hi,
                                         preferred_element_type=F32), 0.0).T

    b_re, b_im = bdiag_b(bre_ref), bdiag_b(bim_ref)
    c_re, c_im = bdiag_c(cre_ref), bdiag_c(cim_ref)
    l_re, l_im = lrow_ref[0, 0:1, :], lrow_ref[0, 1:2, :]
    lc_re, lc_im = lcol_ref[0, :, 0:1], lcol_ref[0, :, 1:2]
    p_re, p_im = jnp.ones_like(l_re), jnp.zeros_like(l_im)
    q_re, q_im = jnp.ones_like(lc_re), jnp.zeros_like(lc_im)
    zero_tile = jnp.zeros((lanes, lanes), BF16)

    def tile(i):
        return slice(i * lanes, (i + 1) * lanes)

    for n in range(tc):
        lb_re = b_re * p_re - b_im * p_im
        lb_im = b_re * p_im + b_im * p_re
        win_ref[0, tile(tc - 1 - n), :ns] = lb_re.astype(BF16)
        win_ref[0, tile(tc - 1 - n), ns:] = lb_im.astype(BF16)
        kern = (jnp.dot(lb_re, c_re, precision=hi, preferred_element_type=F32)
                - jnp.dot(lb_im, c_im, precision=hi, preferred_element_type=F32)).astype(BF16)
        for j in range(tc - n):
            toep_ref[0, tile(j), tile(j + n)] = kern
        if n:
            for t in range(tc - n):
                toep_ref[0, tile(t + n), tile(t)] = zero_tile
        p_re, p_im = p_re * l_re - p_im * l_im, p_re * l_im + p_im * l_re
        q_re, q_im = q_re * lc_re - q_im * lc_im, q_re * lc_im + q_im * lc_re
        wout_ref[0, :ns, tile(n)] = (c_re * q_re - c_im * q_im).astype(BF16)
        wout_ref[0, ns:, tile(n)] = (-(c_re * q_im + c_im * q_re)).astype(BF16)
    lam_ref[0, :, :ns] = p_re
    lam_ref[0, :, ns:] = p_im


def _s5_weights(a_re, a_im, log_dt, b_re, b_im, c_re, c_im, d_skip):
    tc = S5_CHUNK
    g, p = a_re.shape
    gl = GROUPS_PER_LANE_BLOCK
    nlb = g // gl
    ns = gl * p
    dt = jnp.exp(log_dt)[:, None]
    mag = jnp.exp(a_re * dt)
    l_re = mag * jnp.cos(a_im * dt)
    l_im = mag * jnp.sin(a_im * dt)
    den = a_re * a_re + a_im * a_im
    n_re = l_re - 1.0
    n_im = l_im
    f_re = (n_re * a_re + n_im * a_im) / den
    f_im = (n_im * a_re - n_re * a_im) / den
    bb_re = (f_re[..., None] * b_re - f_im[..., None] * b_im).reshape(nlb, ns, S5_GROUP)
    bb_im = (f_re[..., None] * b_im + f_im[..., None] * b_re).reshape(nlb, ns, S5_GROUP)
    lrow = jnp.stack([l_re.reshape(nlb, ns), l_im.reshape(nlb, ns)], axis=1)
    lcol = jnp.stack([l_re.reshape(nlb, ns), l_im.reshape(nlb, ns)], axis=2)
    width = tc * LANES
    blk = lambda *shape: pl.BlockSpec((1,) + shape, lambda k: (k, 0, 0))
    toep, w_in, w_out, lam = pl.pallas_call(
        functools.partial(_s5prep_kernel, tc=tc),
        out_shape=(jax.ShapeDtypeStruct((nlb, width, width), BF16),
                   jax.ShapeDtypeStruct((nlb, width, 2 * ns), BF16),
                   jax.ShapeDtypeStruct((nlb, 2 * ns, width), BF16),
                   jax.ShapeDtypeStruct((nlb, 1, 2 * ns), F32)),
        grid=(nlb,),
        in_specs=[blk(ns, S5_GROUP), blk(ns, S5_GROUP), blk(LANES, p), blk(LANES, p),
                  blk(2, ns), blk(ns, 2)],
        out_specs=(blk(width, width), blk(width, 2 * ns), blk(2 * ns, width),
                   blk(1, 2 * ns)),
        compiler_params=_params(("parallel",)),
        name="s5prep",
    )(bb_re, bb_im, c_re.reshape(nlb, LANES, p), c_im.reshape(nlb, LANES, p), lrow, lcol)
    dtile = jnp.tile(d_skip.reshape(nlb, 1, LANES), (1, 1, tc))
    return toep, w_in, w_out, lam, dtile


def _ret_kernel(q_ref, k_ref, v_ref, g_ref, cos_ref, sin_ref, dec_ref, xi_ref,
                zeta_ref, gam_ref, o_ref, state_ref, *, heads):
    c = pl.program_id(1)

    @pl.when(c == 0)
    def _():
        state_ref[...] = jnp.zeros_like(state_ref)

    dqk = q_ref.shape[1] // heads
    dv = v_ref.shape[1] // heads
    half = dqk // 2
    cos = cos_ref[...]
    sin = sin_ref[...]

    def rot(t):
        t1, t2 = t[:, :half], t[:, half:]
        return jnp.concatenate([t1 * cos - t2 * sin, t1 * sin + t2 * cos], axis=-1)

    for h in range(heads):
        q = rot(q_ref[:, h * dqk:(h + 1) * dqk].astype(F32))
        k = rot(k_ref[:, h * dqk:(h + 1) * dqk].astype(F32)) * (dqk ** -0.5)
        v = v_ref[:, h * dv:(h + 1) * dv]
        qb = q.astype(BF16)
        scores = _dot_nt(qb, k.astype(BF16)) * dec_ref[h]
        st = state_ref[h]
        out = _dot(scores.astype(BF16), v) + _dot(qb, st.astype(BF16)) * xi_ref[h]
        kz = (k * zeta_ref[h]).astype(BF16)
        state_ref[h] = st * gam_ref[h] + _dot_tn(kz, v)
        mean = jnp.mean(out, axis=-1, keepdims=True)
        cen = out - mean
        var = jnp.mean(cen * cen, axis=-1, keepdims=True)
        normed = cen * lax.rsqrt(var + GN_EPS)
        g = g_ref[:, h * dv:(h + 1) * dv].astype(F32)
        o_ref[:, h * dv:(h + 1) * dv] = (g * jax.nn.sigmoid(g) * normed).astype(BF16)


def _retention(proj, d, batch, seq):
    t = proj.shape[0]
    heads = RET_HEADS
    ch = min(RET_CHUNK, seq)
    nc = seq // ch
    dqk = d // 2 // heads
    dv = d // heads
    half = dqk // 2
    inv = ROPE_BASE ** (-jnp.arange(0, dqk, 2, dtype=F32) / dqk)
    ang = jnp.arange(seq, dtype=F32)[:, None] * inv[None, :]
    cos, sin = jnp.cos(ang), jnp.sin(ang)
    log_gamma = jnp.log(1.0 - 2.0 ** (-5.0 - jnp.arange(heads, dtype=F32)))
    idx = jnp.arange(ch, dtype=F32)
    rel = idx[:, None] - idx[None, :]
    dec = jnp.where(rel[None] >= 0,
                    jnp.exp(jnp.maximum(rel, 0.0)[None] * log_gamma[:, None, None]), 0.0)
    xi = jnp.exp((idx + 1.0)[None, :] * log_gamma[:, None])[:, :, None]
    zeta = jnp.exp((ch - 1.0 - idx)[None, :] * log_gamma[:, None])[:, :, None]
    gam = jnp.exp(ch * log_gamma)[:, None, None]
    row = lambda b, c: b * nc + c
    return pl.pallas_call(
        functools.partial(_ret_kernel, heads=heads),
        out_shape=jax.ShapeDtypeStruct((t, d), BF16),
        grid=(batch, nc),
        in_specs=[
            pl.BlockSpec((ch, d // 2), lambda b, c: (row(b, c), 0)),
            pl.BlockSpec((ch, d // 2), lambda b, c: (row(b, c), 1)),
            pl.BlockSpec((ch, d), lambda b, c: (row(b, c), 1)),
            pl.BlockSpec((ch, d), lambda b, c: (row(b, c), 2)),
            pl.BlockSpec((ch, half), lambda b, c: (c, 0)),
            pl.BlockSpec((ch, half), lambda b, c: (c, 0)),
            pl.BlockSpec((heads, ch, ch), lambda b, c: (0, 0, 0)),
            pl.BlockSpec((heads, ch, 1), lambda b, c: (0, 0, 0)),
            pl.BlockSpec((heads, ch, 1), lambda b, c: (0, 0, 0)),
            pl.BlockSpec((heads, 1, 1), lambda b, c: (0, 0, 0)),
        ],
        out_specs=pl.BlockSpec((ch, d), lambda b, c: (row(b, c), 0)),
        scratch_shapes=[pltpu.VMEM((heads, dqk, dv), F32)],
        compiler_params=_params(("parallel", "arbitrary")),
        name="retention",
    )(proj, proj, proj, proj, cos, sin, dec, xi, zeta, gam)


def _mix_kernel(og_ref, yg_ref, ga_ref, gb_ref, h_ref, wo_ref, wv_ref, wg_ref,
                wout_ref, o_ref, ygc_ref, acc_ref):
    j = pl.program_id(1)

    @pl.when(j == 0)
    def _():
        for k in range(yg_ref.shape[0]):
            ygc_ref[:, k * LANES:(k + 1) * LANES] = yg_ref[k]
        acc_ref[...] = jnp.zeros_like(acc_ref)

    yg = ygc_ref[...]
    y_b = _dot(og_ref[...], wo_ref[...])
    y_a = _dot(yg, wv_ref[...]) * jax.nn.sigmoid(_dot(yg, wg_ref[...]))
    merged = (jax.nn.sigmoid(ga_ref[...].astype(F32)) * y_a
              + jax.nn.sigmoid(gb_ref[...].astype(F32)) * y_b)
    acc_ref[...] += _dot(merged.astype(BF16), wout_ref[...])

    @pl.when(j == pl.num_programs(1) - 1)
    def _():
        o_ref[...] = h_ref[...] + acc_ref[...]


def _mix(og, yg, proj, h, w_o, w_v, w_g, w_out, *, tm=512, tn=512):
    t, d = h.shape
    nlb = yg.shape[0]
    sw = nlb * LANES
    tm, tn = min(tm, t), min(tn, d)
    ga0 = 3 * d // tn
    gb0 = 4 * d // tn
    return pl.pallas_call(
        _mix_kernel,
        out_shape=jax.ShapeDtypeStruct((t, d), F32),
        grid=(t // tm, d // tn),
        in_specs=[
            pl.BlockSpec((tm, d), lambda i, j: (i, 0)),
            pl.BlockSpec((nlb, tm, LANES), lambda i, j: (0, i, 0)),
            pl.BlockSpec((tm, tn), lambda i, j: (i, ga0 + j)),
            pl.BlockSpec((tm, tn), lambda i, j: (i, gb0 + j)),
            pl.BlockSpec((tm, d), lambda i, j: (i, 0)),
            pl.BlockSpec((d, tn), lambda i, j: (0, j)),
            pl.BlockSpec((sw, tn), lambda i, j: (0, j)),
            pl.BlockSpec((sw, tn), lambda i, j: (0, j)),
            pl.BlockSpec((tn, d), lambda i, j: (j, 0)),
        ],
        out_specs=pl.BlockSpec((tm, d), lambda i, j: (i, 0)),
        scratch_shapes=[pltpu.VMEM((tm, sw), BF16), pltpu.VMEM((tm, d), F32)],
        compiler_params=_params(("parallel", "arbitrary")),
        name="mix",
    )(og, yg, proj, proj, h, w_o, w_v, w_g, w_out)


def _xattn_kernel(h_ref, g_ref, wq_ref, k_ref, v_ref, wo_ref, o_ref, xn_ref, acc_ref):
    hd = pl.program_id(2)

    @pl.when(hd == 0)
    def _():
        xn_ref[...] = _rms(h_ref[...], g_ref[...]).astype(BF16)
        acc_ref[...] = jnp.zeros_like(acc_ref)

    dh = wq_ref.shape[1]
    q = _dot(xn_ref[...], wq_ref[...])
    s = _dot_nt(q.astype(BF16), k_ref[0]) * (dh ** -0.5)
    e = jnp.exp(s - jnp.max(s, axis=-1, keepdims=True))
    p = e / jnp.sum(e, axis=-1, keepdims=True)
    o = _dot(p.astype(BF16), v_ref[0])
    acc_ref[...] += _dot(o.astype(BF16), wo_ref[...])

    @pl.when(hd == pl.num_programs(2) - 1)
    def _():
        o_ref[...] = h_ref[...] + acc_ref[...]


def _xattn(h, gain, wq, kmem, vmem, wo, batch, *, tm=512):
    t, d = h.shape
    seq = t // batch
    mlen = kmem.shape[1]
    heads = XATTN_HEADS
    dh = d // heads
    tm = min(tm, seq)
    nt = seq // tm
    return pl.pallas_call(
        _xattn_kernel,
        out_shape=jax.ShapeDtypeStruct((t, d), F32),
        grid=(batch, nt, heads),
        in_specs=[
            pl.BlockSpec((tm, d), lambda b, i, hd: (b * nt + i, 0)),
            pl.BlockSpec((1, d), lambda b, i, hd: (0, 0)),
            pl.BlockSpec((d, dh), lambda b, i, hd: (0, hd)),
            pl.BlockSpec((1, mlen, dh), lambda b, i, hd: (b, 0, hd)),
            pl.BlockSpec((1, mlen, dh), lambda b, i, hd: (b, 0, hd)),
            pl.BlockSpec((dh, d), lambda b, i, hd: (hd, 0)),
        ],
        out_specs=pl.BlockSpec((tm, d), lambda b, i, hd: (b * nt + i, 0)),
        scratch_shapes=[pltpu.VMEM((tm, d), BF16), pltpu.VMEM((tm, d), F32)],
        compiler_params=_params(("parallel", "parallel", "arbitrary")),
        name="xattn",
    )(h, gain.reshape(1, d), wq, kmem, vmem, wo)


def kernel(x, mem, ffn1_norm, ffn1_w1, ffn1_w3, ffn1_w2, mix_norm, w_in, s5_a_re, s5_a_im, s5_log_dt, s5_b_re, s5_b_im, s5_c_re, s5_c_im, s5_d, s5_glu_v, s5_glu_g, ret_w_o, w_out, xattn_norm, mem_norm, xattn_wq, xattn_wk, xattn_wv, xattn_wo, ffn2_norm, ffn2_w1, ffn2_w3, ffn2_w2, final_norm):
    batch, seq, d = x.shape
    mlen = mem.shape[1]
    depth = ffn1_w1.shape[0]
    t = batch * seq
    s5_width = s5_d.shape[1]
    nlb = s5_width // LANES
    bf = lambda w: w.astype(BF16)

    h = x.reshape(t, d)
    mem2 = mem.reshape(batch * mlen, d)
    for l in range(depth):
        last = l == depth - 1
        h = _ffn(h, ffn1_norm[l], bf(ffn1_w1[l]), bf(ffn1_w3[l]), bf(ffn1_w2[l]))
        proj, us5 = _inproj(h, mix_norm[l], bf(w_in[l]), s5_width)
        toep, s_in, s_out, lam, dtile = _s5_weights(
            s5_a_re[l], s5_a_im[l], s5_log_dt[l], s5_b_re[l], s5_b_im[l],
            s5_c_re[l], s5_c_im[l], s5_d[l])
        u = us5.reshape(nlb, t // S5_CHUNK, S5_CHUNK * LANES)
        yg = _s5(u, toep, s_in, s_out, lam, dtile, batch).reshape(nlb, t, LANES)
        og = _retention(proj, d, batch, seq)
        h = _mix(og, yg, proj, h, bf(ret_w_o[l]), bf(s5_glu_v[l]), bf(s5_glu_g[l]),
                 bf(w_out[l]))
        kmem = _normmm(mem2, mem_norm[l], bf(xattn_wk[l])).reshape(batch, mlen, d)
        vmem = _normmm(mem2, mem_norm[l], bf(xattn_wv[l])).reshape(batch, mlen, d)
        h = _xattn(h, xattn_norm[l], bf(xattn_wq[l]), kmem, vmem, bf(xattn_wo[l]), batch)
        h = _ffn(h, ffn2_norm[l], bf(ffn2_w1[l]), bf(ffn2_w3[l]), bf(ffn2_w2[l]),
                 final_norm if last else None)
    if depth == 0:
        raise ValueError("depth must be >= 1")
    return h.reshape(batch, seq, d)
```

```python
import functools

import jax
import jax.numpy as jnp
from jax import lax
from jax.experimental import pallas as pl
from jax.experimental.pallas import tpu as pltpu

F32 = jnp.float32
BF16 = jnp.bfloat16

RMS_EPS = 1e-6
GN_EPS = 1e-5
ROPE_BASE = 10000.0
S5_GROUP = 16
RET_HEADS = 4
XATTN_HEADS = 4
RET_CHUNK = 128
S5_CHUNK = 16
LANES = 128
GROUPS_PER_LANE_BLOCK = LANES // S5_GROUP
VMEM_LIMIT_BYTES = 56 * 1024 * 1024


def _params(semantics):
    return pltpu.CompilerParams(dimension_semantics=semantics,
                                vmem_limit_bytes=VMEM_LIMIT_BYTES)


def _rms(x, gain):
    ms = jnp.mean(x * x, axis=-1, keepdims=True)
    return x * lax.rsqrt(ms + RMS_EPS) * gain


def _dot(a, b):
    return jnp.dot(a, b, preferred_element_type=F32)


def _dot_nt(a, b):
    return lax.dot_general(a, b, (((1,), (1,)), ((), ())), preferred_element_type=F32)


def _dot_tn(a, b):
    return lax.dot_general(a, b, (((0,), (0,)), ((), ())), preferred_element_type=F32)


def _ffn_kernel(*refs, final):
    if final:
        x_ref, g_ref, w1_ref, w3_ref, w2_ref, fg_ref, o_ref, xn_ref = refs
    else:
        x_ref, g_ref, w1_ref, w3_ref, w2_ref, o_ref, xn_ref = refs
    j = pl.program_id(1)

    @pl.when(j == 0)
    def _():
        x = x_ref[...]
        xn_ref[...] = _rms(x, g_ref[...]).astype(BF16)
        o_ref[...] = x

    xn = xn_ref[...]
    a = _dot(xn, w1_ref[...])
    b = _dot(xn, w3_ref[...])
    mid = (a * jax.nn.sigmoid(a) * b * 0.5).astype(BF16)
    o_ref[...] += _dot(mid, w2_ref[...])

    if final:
        @pl.when(j == pl.num_programs(1) - 1)
        def _():
            o_ref[...] = _rms(o_ref[...], fg_ref[...])


def _ffn(x, gain, w1, w3, w2, final_gain=None, *, tm=512, tf=512):
    t, d = x.shape
    f = w1.shape[1]
    tm, tf = min(tm, t), min(tf, f)
    final = final_gain is not None
    in_specs = [
        pl.BlockSpec((tm, d), lambda i, j: (i, 0)),
        pl.BlockSpec((1, d), lambda i, j: (0, 0)),
        pl.BlockSpec((d, tf), lambda i, j: (0, j)),
        pl.BlockSpec((d, tf), lambda i, j: (0, j)),
        pl.BlockSpec((tf, d), lambda i, j: (j, 0)),
    ]
    args = [x, gain.reshape(1, d), w1, w3, w2]
    if final:
        in_specs.append(pl.BlockSpec((1, d), lambda i, j: (0, 0)))
        args.append(final_gain.reshape(1, d))
    return pl.pallas_call(
        functools.partial(_ffn_kernel, final=final),
        out_shape=jax.ShapeDtypeStruct((t, d), F32),
        grid=(t // tm, f // tf),
        in_specs=in_specs,
        out_specs=pl.BlockSpec((tm, d), lambda i, j: (i, 0)),
        scratch_shapes=[pltpu.VMEM((tm, d), BF16)],
        compiler_params=_params(("parallel", "arbitrary")),
        name="ffn_final" if final else "ffn",
    )(*args)


def _inproj_kernel(x_ref, g_ref, w_ref, proj_ref, us5_ref, xn_ref):
    j = pl.program_id(1)

    @pl.when(j == 0)
    def _():
        xn_ref[...] = _rms(x_ref[...], g_ref[...]).astype(BF16)

    res = _dot(xn_ref[...], w_ref[...]).astype(BF16)

    @pl.when(j == 0)
    def _():
        for k in range(us5_ref.shape[0]):
            us5_ref[k] = res[:, k * LANES:(k + 1) * LANES]

    @pl.when(j > 0)
    def _():
        proj_ref[...] = res


def _inproj(x, gain, w, s5_width, *, tm=1024):
    t, d = x.shape
    n = w.shape[1]
    tn = s5_width
    tm = min(tm, t)
    nlb = s5_width // LANES
    return pl.pallas_call(
        _inproj_kernel,
        out_shape=(jax.ShapeDtypeStruct((t, n - tn), BF16),
                   jax.ShapeDtypeStruct((nlb, t, LANES), BF16)),
        grid=(t // tm, n // tn),
        in_specs=[
            pl.BlockSpec((tm, d), lambda i, j: (i, 0)),
            pl.BlockSpec((1, d), lambda i, j: (0, 0)),
            pl.BlockSpec((d, tn), lambda i, j: (0, j)),
        ],
        out_specs=(
            pl.BlockSpec((tm, tn), lambda i, j: (i, jnp.maximum(j - 1, 0))),
            pl.BlockSpec((nlb, tm, LANES), lambda i, j: (0, i, 0)),
        ),
        scratch_shapes=[pltpu.VMEM((tm, d), BF16)],
        compiler_params=_params(("parallel", "arbitrary")),
        name="inproj",
    )(x, gain.reshape(1, d), w)


def _normmm_kernel(x_ref, g_ref, w_ref, o_ref):
    xn = _rms(x_ref[...], g_ref[...]).astype(BF16)
    o_ref[...] = _dot(xn, w_ref[...]).astype(BF16)


def _normmm(x, gain, w, *, tn=512):
    t, d = x.shape
    n = w.shape[1]
    tn = min(tn, n)
    return pl.pallas_call(
        _normmm_kernel,
        out_shape=jax.ShapeDtypeStruct((t, n), BF16),
        grid=(n // tn,),
        in_specs=[
            pl.BlockSpec((t, d), lambda j: (0, 0)),
            pl.BlockSpec((1, d), lambda j: (0, 0)),
            pl.BlockSpec((d, tn), lambda j: (0, j)),
        ],
        out_specs=pl.BlockSpec((t, tn), lambda j: (0, j)),
        compiler_params=_params(("parallel",)),
        name="normmm",
    )(x, gain.reshape(1, d), w)


def _s5_kernel(u_ref, m_ref, win_ref, wout_ref, lam_ref, d_ref, y_ref, z_ref, s_ref):
    u = u_ref[0]
    z_ref[...] = _dot(u, win_ref[0])
    ns = z_ref.shape[1] // 2
    lam_re = lam_ref[0, :, :ns]
    lam_im = lam_ref[0, :, ns:]

    def step(c, carry):
        s_re, s_im = carry
        s_ref[pl.ds(c, 1), :ns] = s_re
        s_ref[pl.ds(c, 1), ns:] = s_im
        z_re = z_ref[pl.ds(c, 1), :ns]
        z_im = z_ref[pl.ds(c, 1), ns:]
        return (lam_re * s_re - lam_im * s_im + z_re,
                lam_re * s_im + lam_im * s_re + z_im)

    zero = jnp.zeros((1, ns), F32)
    lax.fori_loop(0, z_ref.shape[0], step, (zero, zero))

    y = (_dot(u, m_ref[0]) + _dot(s_ref[...].astype(BF16), wout_ref[0])
         + d_ref[0] * u.astype(F32))
    y_ref[0] = jax.nn.gelu(y).astype(BF16)


def _s5(u, m, w_in, w_out, lam, dtile, batch):
    nlb, rows, width = u.shape
    rb = rows // batch
    nstate = w_in.shape[2]
    return pl.pallas_call(
        _s5_kernel,
        out_shape=jax.ShapeDtypeStruct((nlb, rows, width), BF16),
        grid=(nlb, batch),
        in_specs=[
            pl.BlockSpec((1, rb, width), lambda k, b: (k, b, 0)),
            pl.BlockSpec((1, width, width), lambda k, b: (k, 0, 0)),
            pl.BlockSpec((1, width, nstate), lambda k, b: (k, 0, 0)),
            pl.BlockSpec((1, nstate, width), lambda k, b: (k, 0, 0)),
            pl.BlockSpec((1, 1, nstate), lambda k, b: (k, 0, 0)),
            pl.BlockSpec((1, 1, width), lambda k, b: (k, 0, 0)),
        ],
        out_specs=pl.BlockSpec((1, rb, width), lambda k, b: (k, b, 0)),
        scratch_shapes=[pltpu.VMEM((rb, nstate), F32), pltpu.VMEM((rb, nstate), F32)],
        compiler_params=_params(("parallel", "arbitrary")),
        name="s5",
    )(u, m, w_in, w_out, lam, dtile)


def _log2(n):
    assert n & (n - 1) == 0, n
    return n.bit_length() - 1


def _s5prep_kernel(bre_ref, bim_ref, cre_ref, cim_ref, lrow_ref, lcol_ref,
                   toep_ref, win_ref, wout_ref, lam_ref, *, tc):
    hi = lax.Precision.HIGHEST
    ns, gi = bre_ref.shape[1], bre_ref.shape[2]
    lanes, p = cre_ref.shape[1], cre_ref.shape[2]

    def iota(shape, dim):
        return lax.broadcasted_iota(jnp.int32, shape, dim)

    def grp(x, size):
        return lax.shift_right_logical(x, _log2(size))

    e_i = ((iota((gi, lanes), 1) & (gi - 1)) == iota((gi, lanes), 0)).astype(F32)
    e_p = ((iota((p, ns), 1) & (p - 1)) == iota((p, ns), 0)).astype(F32)
    mask_b = grp(iota((ns, lanes), 0), p) == grp(iota((ns, lanes), 1), gi)
    mask_c = grp(iota((lanes, ns), 0), gi) == grp(iota((lanes, ns), 1), p)

    def bdiag_b(ref):
        return jnp.where(mask_b, jnp.dot(ref[0], e_i, precision=hi,
                                         preferred_element_type=F32), 0.0).T

    def bdiag_c(ref):
        return jnp.where(mask_c, jnp.dot(ref[0], e_p, precision=hi,
                                         preferred_element_type=F32), 0.0).T

    b_re, b_im = bdiag_b(bre_ref), bdiag_b(bim_ref)
    c_re, c_im = bdiag_c(cre_ref), bdiag_c(cim_ref)
    l_re, l_im = lrow_ref[0, 0:1, :], lrow_ref[0, 1:2, :]
    lc_re, lc_im = lcol_ref[0, :, 0:1], lcol_ref[0, :, 1:2]
    p_re, p_im = jnp.ones_like(l_re), jnp.zeros_like(l_im)
    q_re, q_im = jnp.ones_like(lc_re), jnp.zeros_like(lc_im)
    zero_tile = jnp.zeros((lanes, lanes), BF16)

    def tile(i):
        return slice(i * lanes, (i + 1) * lanes)

    for n in range(tc):
        lb_re = b_re * p_re - b_im * p_im
        lb_im = b_re * p_im + b_im * p_re
        win_ref[0, tile(tc - 1 - n), :ns] = lb_re.astype(BF16)
        win_ref[0, tile(tc - 1 - n), ns:] = lb_im.astype(BF16)
        kern = (jnp.dot(lb_re, c_re, precision=hi, preferred_element_type=F32)
                - jnp.dot(lb_im, c_im, precision=hi, preferred_element_type=F32)).astype(BF16)
        for j in range(tc - n):
            toep_ref[0, tile(j), tile(j + n)] = kern
        if n:
            for t in range(tc - n):
                toep_ref[0, tile(t + n), tile(t)] = zero_tile
        p_re, p_im = p_re * l_re - p_im * l_im, p_re * l_im + p_im * l_re
        q_re, q_im = q_re * lc_re - q_im * lc_im, q_re * lc_im + q_im * lc_re
        wout_ref[0, :ns, tile(n)] = (c_re * q_re - c_im * q_im).astype(BF16)
        wout_ref[0, ns:, tile(n)] = (-(c_re * q_im + c_im * q_re)).astype(BF16)
    lam_ref[0, :, :ns] = p_re
    lam_ref[0, :, ns:] = p_im


def _s5_weights(a_re, a_im, log_dt, b_re, b_im, c_re, c_im, d_skip):
    tc = S5_CHUNK
    g, p = a_re.shape
    gl = GROUPS_PER_LANE_BLOCK
    nlb = g // gl
    ns = gl * p
    dt = jnp.exp(log_dt)[:, None]
    mag = jnp.exp(a_re * dt)
    l_re = mag * jnp.cos(a_im * dt)
    l_im = mag * jnp.sin(a_im * dt)
    den = a_re * a_re + a_im * a_im
    n_re = l_re - 1.0
    n_im = l_im
    f_re = (n_re * a_re + n_im * a_im) / den
    f_im = (n_im * a_re - n_re * a_im) / den
    bb_re = (f_re[..., None] * b_re - f_im[..., None] * b_im).reshape(nlb, ns, S5_GROUP)
    bb_im = (f_re[..., None] * b_im + f_im[..., None] * b_re).reshape(nlb, ns, S5_GROUP)
    lrow = jnp.stack([l_re.reshape(nlb, ns), l_im.reshape(nlb, ns)], axis=1)
    lcol = jnp.stack([l_re.reshape(nlb, ns), l_im.reshape(nlb, ns)], axis=2)
    width = tc * LANES
    blk = lambda *shape: pl.BlockSpec((1,) + shape, lambda k: (k, 0, 0))
    toep, w_in, w_out, lam = pl.pallas_call(
        functools.partial(_s5prep_kernel, tc=tc),
        out_shape=(jax.ShapeDtypeStruct((nlb, width, width), BF16),
                   jax.ShapeDtypeStruct((nlb, width, 2 * ns), BF16),
                   jax.ShapeDtypeStruct((nlb, 2 * ns, width), BF16),
                   jax.ShapeDtypeStruct((nlb, 1, 2 * ns), F32)),
        grid=(nlb,),
        in_specs=[blk(ns, S5_GROUP), blk(ns, S5_GROUP), blk(LANES, p), blk(LANES, p),
                  blk(2, ns), blk(ns, 2)],
        out_specs=(blk(width, width), blk(width, 2 * ns), blk(2 * ns, width),
                   blk(1, 2 * ns)),
        compiler_params=_params(("parallel",)),
        name="s5prep",
    )(bb_re, bb_im, c_re.reshape(nlb, LANES, p), c_im.reshape(nlb, LANES, p), lrow, lcol)
    dtile = jnp.tile(d_skip.reshape(nlb, 1, LANES), (1, 1, tc))
    return toep, w_in, w_out, lam, dtile


def _ret_kernel(q_ref, k_ref, v_ref, g_ref, cos_ref, sin_ref, dec_ref, xi_ref,
                zeta_ref, gam_ref, o_ref, state_ref, *, heads):
    c = pl.program_id(1)

    @pl.when(c == 0)
    def _():
        state_ref[...] = jnp.zeros_like(state_ref)

    dqk = q_ref.shape[1] // heads
    dv = v_ref.shape[1] // heads
    half = dqk // 2
    cos = cos_ref[...]
    sin = sin_ref[...]

    def rot(t):
        t1, t2 = t[:, :half], t[:, half:]
        return jnp.concatenate([t1 * cos - t2 * sin, t1 * sin + t2 * cos], axis=-1)

    for h in range(heads):
        q = rot(q_ref[:, h * dqk:(h + 1) * dqk].astype(F32))
        k = rot(k_ref[:, h * dqk:(h + 1) * dqk].astype(F32)) * (dqk ** -0.5)
        v = v_ref[:, h * dv:(h + 1) * dv]
        qb = q.astype(BF16)
        scores = _dot_nt(qb, k.astype(BF16)) * dec_ref[h]
        st = state_ref[h]
        out = _dot(scores.astype(BF16), v) + _dot(qb, st.astype(BF16)) * xi_ref[h]
        kz = (k * zeta_ref[h]).astype(BF16)
        state_ref[h] = st * gam_ref[h] + _dot_tn(kz, v)
        mean = jnp.mean(out, axis=-1, keepdims=True)
        cen = out - mean
        var = jnp.mean(cen * cen, axis=-1, keepdims=True)
        normed = cen * lax.rsqrt(var + GN_EPS)
        g = g_ref[:, h * dv:(h + 1) * dv].astype(F32)
        o_ref[:, h * dv:(h + 1) * dv] = (g * jax.nn.sigmoid(g) * normed).astype(BF16)


def _retention(proj, d, batch, seq):
    t = proj.shape[0]
    heads = RET_HEADS
    ch = min(RET_CHUNK, seq)
    nc = seq // ch
    dqk = d // 2 // heads
    dv = d // heads
    half = dqk // 2
    inv = ROPE_BASE ** (-jnp.arange(0, dqk, 2, dtype=F32) / dqk)
    ang = jnp.arange(seq, dtype=F32)[:, None] * inv[None, :]
    cos, sin = jnp.cos(ang), jnp.sin(ang)
    log_gamma = jnp.log(1.0 - 2.0 ** (-5.0 - jnp.arange(heads, dtype=F32)))
    idx = jnp.arange(ch, dtype=F32)
    rel = idx[:, None] - idx[None, :]
    dec = jnp.where(rel[None] >= 0,
                    jnp.exp(jnp.maximum(rel, 0.0)[None] * log_gamma[:, None, None]), 0.0)
    xi = jnp.exp((idx + 1.0)[None, :] * log_gamma[:, None])[:, :, None]
    zeta = jnp.exp((ch - 1.0 - idx)[None, :] * log_gamma[:, None])[:, :, None]
    gam = jnp.exp(ch * log_gamma)[:, None, None]
    row = lambda b, c: b * nc + c
    return pl.pallas_call(
        functools.partial(_ret_kernel, heads=heads),
        out_shape=jax.ShapeDtypeStruct((t, d), BF16),
        grid=(batch, nc),
        in_specs=[
            pl.BlockSpec((ch, d // 2), lambda b, c: (row(b, c), 0)),
            pl.BlockSpec((ch, d // 2), lambda b, c: (row(b, c), 1)),
            pl.BlockSpec((ch, d), lambda b, c: (row(b, c), 1)),
            pl.BlockSpec((ch, d), lambda b, c: (row(b, c), 2)),
            pl.BlockSpec((ch, half), lambda b, c: (c, 0)),
            pl.BlockSpec((ch, half), lambda b, c: (c, 0)),
            pl.BlockSpec((heads, ch, ch), lambda b, c: (0, 0, 0)),
            pl.BlockSpec((heads, ch, 1), lambda b, c: (0, 0, 0)),
            pl.BlockSpec((heads, ch, 1), lambda b, c: (0, 0, 0)),
            pl.BlockSpec((heads, 1, 1), lambda b, c: (0, 0, 0)),
        ],
        out_specs=pl.BlockSpec((ch, d), lambda b, c: (row(b, c), 0)),
        scratch_shapes=[pltpu.VMEM((heads, dqk, dv), F32)],
        compiler_params=_params(("parallel", "arbitrary")),
        name="retention",
    )(proj, proj, proj, proj, cos, sin, dec, xi, zeta, gam)


def _resident(shape):
    return pl.BlockSpec(shape, lambda *_: (0,) * len(shape), pipeline_mode=pl.Buffered(1))


def _mix_kernel(og_ref, yg_ref, ga_ref, gb_ref, h_ref, wo_ref, wv_ref, wg_ref,
                wout_ref, o_ref, *, cn):
    d = h_ref.shape[1]
    og = og_ref[...]
    yg = jnp.concatenate([yg_ref[k] for k in range(yg_ref.shape[0])], axis=-1)
    acc = h_ref[...]
    for j in range(d // cn):
        cols = slice(j * cn, (j + 1) * cn)
        y_b = _dot(og, wo_ref[:, cols])
        y_a = _dot(yg, wv_ref[:, cols]) * jax.nn.sigmoid(_dot(yg, wg_ref[:, cols]))
        merged = (jax.nn.sigmoid(ga_ref[:, cols].astype(F32)) * y_a
                  + jax.nn.sigmoid(gb_ref[:, cols].astype(F32)) * y_b)
        acc = acc + _dot(merged.astype(BF16), wout_ref[cols, :])
    o_ref[...] = acc


def _mix(og, yg, proj, h, w_o, w_v, w_g, w_out, *, tm=256, cn=512):
    t, d = h.shape
    nlb = yg.shape[0]
    sw = nlb * LANES
    tm, cn = min(tm, t), min(cn, d)
    return pl.pallas_call(
        functools.partial(_mix_kernel, cn=cn),
        out_shape=jax.ShapeDtypeStruct((t, d), F32),
        grid=(t // tm,),
        in_specs=[
            pl.BlockSpec((tm, d), lambda i: (i, 0)),
            pl.BlockSpec((nlb, tm, LANES), lambda i: (0, i, 0)),
            pl.BlockSpec((tm, d), lambda i: (i, 3)),
            pl.BlockSpec((tm, d), lambda i: (i, 4)),
            pl.BlockSpec((tm, d), lambda i: (i, 0)),
            _resident((d, d)), _resident((sw, d)), _resident((sw, d)), _resident((d, d)),
        ],
        out_specs=pl.BlockSpec((tm, d), lambda i: (i, 0)),
        compiler_params=_params(("arbitrary",)),
        name="mix",
    )(og, yg, proj, proj, h, w_o, w_v, w_g, w_out)


def _xattn_kernel(h_ref, g_ref, wq_ref, k_ref, v_ref, wo_ref, o_ref, *, heads):
    h = h_ref[...]
    d = h.shape[1]
    dh = d // heads
    xn = _rms(h, g_ref[...]).astype(BF16)
    outs = []
    for hd in range(heads):
        cols = slice(hd * dh, (hd + 1) * dh)
        q = _dot(xn, wq_ref[:, cols])
        s = _dot_nt(q.astype(BF16), k_ref[0, :, cols]) * (dh ** -0.5)
        e = jnp.exp(s - jnp.max(s, axis=-1, keepdims=True))
        p = e / jnp.sum(e, axis=-1, keepdims=True)
        outs.append(_dot(p.astype(BF16), v_ref[0, :, cols]).astype(BF16))
    o_ref[...] = h + _dot(jnp.concatenate(outs, axis=-1), wo_ref[...])


def _xattn(h, gain, wq, kmem, vmem, wo, batch, *, tm=512):
    t, d = h.shape
    seq = t // batch
    mlen = kmem.shape[1]
    tm = min(tm, seq)
    nt = seq // tm
    return pl.pallas_call(
        functools.partial(_xattn_kernel, heads=XATTN_HEADS),
        out_shape=jax.ShapeDtypeStruct((t, d), F32),
        grid=(batch, nt),
        in_specs=[
            pl.BlockSpec((tm, d), lambda b, i: (b * nt + i, 0)),
            _resident((1, d)),
            _resident((d, d)),
            pl.BlockSpec((1, mlen, d), lambda b, i: (b, 0, 0)),
            pl.BlockSpec((1, mlen, d), lambda b, i: (b, 0, 0)),
            _resident((d, d)),
        ],
        out_specs=pl.BlockSpec((tm, d), lambda b, i: (b * nt + i, 0)),
        compiler_params=_params(("arbitrary", "arbitrary")),
        name="xattn",
    )(h, gain.reshape(1, d), wq, kmem, vmem, wo)


def kernel(x, mem, ffn1_norm, ffn1_w1, ffn1_w3, ffn1_w2, mix_norm, w_in, s5_a_re, s5_a_im, s5_log_dt, s5_b_re, s5_b_im, s5_c_re, s5_c_im, s5_d, s5_glu_v, s5_glu_g, ret_w_o, w_out, xattn_norm, mem_norm, xattn_wq, xattn_wk, xattn_wv, xattn_wo, ffn2_norm, ffn2_w1, ffn2_w3, ffn2_w2, final_norm):
    batch, seq, d = x.shape
    mlen = mem.shape[1]
    depth = ffn1_w1.shape[0]
    t = batch * seq
    s5_width = s5_d.shape[1]
    nlb = s5_width // LANES
    bf = lambda w: w.astype(BF16)

    h = x.reshape(t, d)
    mem2 = mem.reshape(batch * mlen, d)
    for l in range(depth):
        last = l == depth - 1
        h = _ffn(h, ffn1_norm[l], bf(ffn1_w1[l]), bf(ffn1_w3[l]), bf(ffn1_w2[l]))
        proj, us5 = _inproj(h, mix_norm[l], bf(w_in[l]), s5_width)
        toep, s_in, s_out, lam, dtile = _s5_weights(
            s5_a_re[l], s5_a_im[l], s5_log_dt[l], s5_b_re[l], s5_b_im[l],
            s5_c_re[l], s5_c_im[l], s5_d[l])
        u = us5.reshape(nlb, t // S5_CHUNK, S5_CHUNK * LANES)
        yg = _s5(u, toep, s_in, s_out, lam, dtile, batch).reshape(nlb, t, LANES)
        og = _retention(proj, d, batch, seq)
        h = _mix(og, yg, proj, h, bf(ret_w_o[l]), bf(s5_glu_v[l]), bf(s5_glu_g[l]),
                 bf(w_out[l]))
        kmem = _normmm(mem2, mem_norm[l], bf(xattn_wk[l])).reshape(batch, mlen, d)
        vmem = _normmm(mem2, mem_norm[l], bf(xattn_wv[l])).reshape(batch, mlen, d)
        h = _xattn(h, xattn_norm[l], bf(xattn_wq[l]), kmem, vmem, bf(xattn_wo[l]), batch)
        h = _ffn(h, ffn2_norm[l], bf(ffn2_w1[l]), bf(ffn2_w3[l]), bf(ffn2_w2[l]),
                 final_norm if last else None)
    if depth == 0:
        raise ValueError("depth must be >= 1")
    return h.reshape(batch, seq, d)
```

```python
import functools

import jax
import jax.numpy as jnp
from jax import lax
from jax.experimental import pallas as pl
from jax.experimental.pallas import tpu as pltpu

F32 = jnp.float32
BF16 = jnp.bfloat16

RMS_EPS = 1e-6
GN_EPS = 1e-5
ROPE_BASE = 10000.0
S5_GROUP = 16
RET_HEADS = 4
XATTN_HEADS = 4
RET_KERNEL_CHUNK = 256
S5_CHUNK = 16
LANES = 128
GROUPS_PER_LANE_BLOCK = LANES // S5_GROUP
VMEM_LIMIT_BYTES = 60 * 1024 * 1024
MXU_TILE = 256


def _params(semantics):
    return pltpu.CompilerParams(dimension_semantics=semantics,
                                vmem_limit_bytes=VMEM_LIMIT_BYTES)


def _rms(x, gain):
    ms = jnp.mean(x * x, axis=-1, keepdims=True)
    return x * lax.rsqrt(ms + RMS_EPS) * gain


def _dot(a, b):
    return jnp.dot(a, b, preferred_element_type=F32)


def _dot_nt(a, b):
    return lax.dot_general(a, b, (((1,), (1,)), ((), ())), preferred_element_type=F32)


def _dot_tn(a, b):
    return lax.dot_general(a, b, (((0,), (0,)), ((), ())), preferred_element_type=F32)


def _ffn_kernel(*refs, final):
    if final:
        x_ref, g_ref, w1_ref, w3_ref, w2_ref, fg_ref, o_ref, xn_ref = refs
    else:
        x_ref, g_ref, w1_ref, w3_ref, w2_ref, o_ref, xn_ref = refs
    j = pl.program_id(1)

    @pl.when(j == 0)
    def _():
        x = x_ref[...]
        xn_ref[...] = _rms(x, g_ref[...]).astype(BF16)
        o_ref[...] = x

    xn = xn_ref[...]
    a = _dot(xn, w1_ref[...])
    b = _dot(xn, w3_ref[...])
    mid = (a * jax.nn.sigmoid(a) * b * 0.5).astype(BF16)
    o_ref[...] += _dot(mid, w2_ref[...])

    if final:
        @pl.when(j == pl.num_programs(1) - 1)
        def _():
            o_ref[...] = _rms(o_ref[...], fg_ref[...])


def _ffn(x, gain, w1, w3, w2, final_gain=None, *, tm=1024, tf=512):
    t, d = x.shape
    f = w1.shape[1]
    tm, tf = min(tm, t), min(tf, f)
    final = final_gain is not None
    in_specs = [
        pl.BlockSpec((tm, d), lambda i, j: (i, 0)),
        pl.BlockSpec((1, d), lambda i, j: (0, 0)),
        pl.BlockSpec((d, tf), lambda i, j: (0, j)),
        pl.BlockSpec((d, tf), lambda i, j: (0, j)),
        pl.BlockSpec((tf, d), lambda i, j: (j, 0)),
    ]
    args = [x, gain.reshape(1, d), w1, w3, w2]
    if final:
        in_specs.append(pl.BlockSpec((1, d), lambda i, j: (0, 0)))
        args.append(final_gain.reshape(1, d))
    return pl.pallas_call(
        functools.partial(_ffn_kernel, final=final),
        out_shape=jax.ShapeDtypeStruct((t, d), F32),
        grid=(t // tm, f // tf),
        in_specs=in_specs,
        out_specs=pl.BlockSpec((tm, d), lambda i, j: (i, 0)),
        scratch_shapes=[pltpu.VMEM((tm, d), BF16)],
        compiler_params=_params(("parallel", "arbitrary")),
        name="ffn_final" if final else "ffn",
    )(*args)


def _inproj_kernel(x_ref, g_ref, w_ref, proj_ref, us5_ref, xn_ref):
    j = pl.program_id(1)

    @pl.when(j == 0)
    def _():
        xn_ref[...] = _rms(x_ref[...], g_ref[...]).astype(BF16)

    res = _dot(xn_ref[...], w_ref[...])

    @pl.when(j == 0)
    def _():
        us5_ref[...] = res

    @pl.when(j > 0)
    def _():
        proj_ref[...] = res.astype(BF16)


def _inproj(x, gain, w, s5_width, *, tm=1024):
    t, d = x.shape
    n = w.shape[1]
    tn = s5_width
    tm = min(tm, t)
    return pl.pallas_call(
        _inproj_kernel,
        out_shape=(jax.ShapeDtypeStruct((t, n - tn), BF16),
                   jax.ShapeDtypeStruct((t, tn), F32)),
        grid=(t // tm, n // tn),
        in_specs=[
            pl.BlockSpec((tm, d), lambda i, j: (i, 0)),
            pl.BlockSpec((1, d), lambda i, j: (0, 0)),
            pl.BlockSpec((d, tn), lambda i, j: (0, j)),
        ],
        out_specs=(
            pl.BlockSpec((tm, tn), lambda i, j: (i, jnp.maximum(j - 1, 0))),
            pl.BlockSpec((tm, tn), lambda i, j: (i, 0)),
        ),
        scratch_shapes=[pltpu.VMEM((tm, d), BF16)],
        compiler_params=_params(("parallel", "arbitrary")),
        name="inproj",
    )(x, gain.reshape(1, d), w)


def _normmm_kernel(x_ref, g_ref, w_ref, o_ref):
    xn = _rms(x_ref[...], g_ref[...]).astype(BF16)
    o_ref[...] = _dot(xn, w_ref[...]).astype(BF16)


def _normmm(x, gain, w, *, tn=512):
    t, d = x.shape
    n = w.shape[1]
    tn = min(tn, n)
    return pl.pallas_call(
        _normmm_kernel,
        out_shape=jax.ShapeDtypeStruct((t, n), BF16),
        grid=(n // tn,),
        in_specs=[
            pl.BlockSpec((t, d), lambda j: (0, 0)),
            pl.BlockSpec((1, d), lambda j: (0, 0)),
            pl.BlockSpec((d, tn), lambda j: (0, j)),
        ],
        out_specs=pl.BlockSpec((t, tn), lambda j: (0, j)),
        compiler_params=_params(("parallel",)),
        name="normmm",
    )(x, gain.reshape(1, d), w)


def _s5_kernel(u_ref, m_ref, win_ref, wout_ref, lam_ref, d_ref, y_ref, z_ref, s_ref, *, tc):
    rb = z_ref.shape[0]
    u32 = jnp.concatenate([u_ref[pl.ds(j, rb, stride=tc), :] for j in range(tc)], axis=-1)
    u = u32.astype(BF16)
    z_ref[...] = _dot(u, win_ref[0])
    ns = z_ref.shape[1] // 2
    lam_re = lam_ref[0, :, :ns]
    lam_im = lam_ref[0, :, ns:]

    def step(c, carry):
        s_re, s_im = carry
        s_ref[pl.ds(c, 1), :ns] = s_re
        s_ref[pl.ds(c, 1), ns:] = s_im
        z_re = z_ref[pl.ds(c, 1), :ns]
        z_im = z_ref[pl.ds(c, 1), ns:]
        return (lam_re * s_re - lam_im * s_im + z_re,
                lam_re * s_im + lam_im * s_re + z_im)

    zero = jnp.zeros((1, ns), F32)
    lax.fori_loop(0, z_ref.shape[0], step, (zero, zero))

    s = s_ref[...].astype(BF16)
    per_tile = MXU_TILE // LANES
    for cb in range(tc // per_tile):
        cols = slice(cb * MXU_TILE, (cb + 1) * MXU_TILE)
        kmax = (cb + 1) * MXU_TILE
        y = (_dot(u[:, :kmax], m_ref[0, :kmax, cols]) + _dot(s, wout_ref[0, :, cols])
             + d_ref[0, :, cols] * u32[:, cols])
        y = jax.nn.gelu(y)
        for i in range(per_tile):
            y_ref[pl.ds(cb * per_tile + i, rb, stride=tc), :] = y[:, i * LANES:(i + 1) * LANES]


def _s5(u, m, w_in, w_out, lam, dtile, batch):
    t, sw = u.shape
    seq = t // batch
    tc = S5_CHUNK
    nlb, width, nstate = w_in.shape
    rb = seq // tc
    return pl.pallas_call(
        functools.partial(_s5_kernel, tc=tc),
        out_shape=jax.ShapeDtypeStruct((t, sw), F32),
        grid=(nlb, batch),
        in_specs=[
            pl.BlockSpec((seq, LANES), lambda k, b: (b, k)),
            pl.BlockSpec((1, width, width), lambda k, b: (k, 0, 0)),
            pl.BlockSpec((1, width, nstate), lambda k, b: (k, 0, 0)),
            pl.BlockSpec((1, nstate, width), lambda k, b: (k, 0, 0)),
            pl.BlockSpec((1, 1, nstate), lambda k, b: (k, 0, 0)),
            pl.BlockSpec((1, 1, width), lambda k, b: (k, 0, 0)),
        ],
        out_specs=pl.BlockSpec((seq, LANES), lambda k, b: (b, k)),
        scratch_shapes=[pltpu.VMEM((rb, nstate), F32), pltpu.VMEM((rb, nstate), F32)],
        compiler_params=_params(("parallel", "arbitrary")),
        name="s5",
    )(u, m, w_in, w_out, lam, dtile)


def _log2(n):
    assert n & (n - 1) == 0, n
    return n.bit_length() - 1


def _s5prep_kernel(bre_ref, bim_ref, cre_ref, cim_ref, lrow_ref, lcol_ref,
                   toep_ref, win_ref, wout_ref, lam_ref, *, tc):
    hi = lax.Precision.HIGHEST
    ns, gi = bre_ref.shape[1], bre_ref.shape[2]
    lanes, p = cre_ref.shape[1], cre_ref.shape[2]

    def iota(shape, dim):
        return lax.broadcasted_iota(jnp.int32, shape, dim)

    def grp(x, size):
        return lax.shift_right_logical(x, _log2(size))

    e_i = ((iota((gi, lanes), 1) & (gi - 1)) == iota((gi, lanes), 0)).astype(F32)
    e_p = ((iota((p, ns), 1) & (p - 1)) == iota((p, ns), 0)).astype(F32)
    mask_b = grp(iota((ns, lanes), 0), p) == grp(iota((ns, lanes), 1), gi)
    mask_c = grp(iota((lanes, ns), 0), gi) == grp(iota((lanes, ns), 1), p)

    def bdiag_b(ref):
        return jnp.where(mask_b, jnp.dot(ref[0], e_i, precision=hi,
                                         preferred_element_type=F32), 0.0).T

    def bdiag_c(ref):
        return jnp.where(mask_c, jnp.dot(ref[0], e_p, precision=hi,
                                         preferred_element_type=F32), 0.0).T

    b_re, b_im = bdiag_b(bre_ref), bdiag_b(bim_ref)
    c_re, c_im = bdiag_c(cre_ref), bdiag_c(cim_ref)
    l_re, l_im = lrow_ref[0, 0:1, :], lrow_ref[0, 1:2, :]
    lc_re, lc_im = lcol_ref[0, :, 0:1], lcol_ref[0, :, 1:2]
    p_re, p_im = jnp.ones_like(l_re), jnp.zeros_like(l_im)
    q_re, q_im = jnp.ones_like(lc_re), jnp.zeros_like(lc_im)
    zero_tile = jnp.zeros((lanes, lanes), BF16)

    def tile(i):
        return slice(i * lanes, (i + 1) * lanes)

    for n in range(tc):
        lb_re = b_re * p_re - b_im * p_im
        lb_im = b_re * p_im + b_im * p_re
        win_ref[0, tile(tc - 1 - n), :ns] = lb_re.astype(BF16)
        win_ref[0, tile(tc - 1 - n), ns:] = lb_im.astype(BF16)
        kern = (jnp.dot(lb_re, c_re, precision=hi, preferred_element_type=F32)
                - jnp.dot(lb_im, c_im, precision=hi, preferred_element_type=F32)).astype(BF16)
        for j in range(tc - n):
            toep_ref[0, tile(j), tile(j + n)] = kern
        if n:
            for t in range(tc - n):
                toep_ref[0, tile(t + n), tile(t)] = zero_tile
        p_re, p_im = p_re * l_re - p_im * l_im, p_re * l_im + p_im * l_re
        q_re, q_im = q_re * lc_re - q_im * lc_im, q_re * lc_im + q_im * lc_re
        wout_ref[0, :ns, tile(n)] = (c_re * q_re - c_im * q_im).astype(BF16)
        wout_ref[0, ns:, tile(n)] = (-(c_re * q_im + c_im * q_re)).astype(BF16)
    lam_ref[0, :, :ns] = p_re
    lam_ref[0, :, ns:] = p_im


def _s5_weights(a_re, a_im, log_dt, b_re, b_im, c_re, c_im, d_skip):
    tc = S5_CHUNK
    g, p = a_re.shape
    gl = GROUPS_PER_LANE_BLOCK
    nlb = g // gl
    ns = gl * p
    dt = jnp.exp(log_dt)[:, None]
    mag = jnp.exp(a_re * dt)
    l_re = mag * jnp.cos(a_im * dt)
    l_im = mag * jnp.sin(a_im * dt)
    den = a_re * a_re + a_im * a_im
    n_re = l_re - 1.0
    n_im = l_im
    f_re = (n_re * a_re + n_im * a_im) / den
    f_im = (n_im * a_re - n_re * a_im) / den
    bb_re = (f_re[..., None] * b_re - f_im[..., None] * b_im).reshape(nlb, ns, S5_GROUP)
    bb_im = (f_re[..., None] * b_im + f_im[..., None] * b_re).reshape(nlb, ns, S5_GROUP)
    lrow = jnp.stack([l_re.reshape(nlb, ns), l_im.reshape(nlb, ns)], axis=1)
    lcol = jnp.stack([l_re.reshape(nlb, ns), l_im.reshape(nlb, ns)], axis=2)
    width = tc * LANES
    blk = lambda *shape: pl.BlockSpec((1,) + shape, lambda k: (k, 0, 0))
    toep, w_in, w_out, lam = pl.pallas_call(
        functools.partial(_s5prep_kernel, tc=tc),
        out_shape=(jax.ShapeDtypeStruct((nlb, width, width), BF16),
                   jax.ShapeDtypeStruct((nlb, width, 2 * ns), BF16),
                   jax.ShapeDtypeStruct((nlb, 2 * ns, width), BF16),
                   jax.ShapeDtypeStruct((nlb, 1, 2 * ns), F32)),
        grid=(nlb,),
        in_specs=[blk(ns, S5_GROUP), blk(ns, S5_GROUP), blk(LANES, p), blk(LANES, p),
                  blk(2, ns), blk(ns, 2)],
        out_specs=(blk(width, width), blk(width, 2 * ns), blk(2 * ns, width),
                   blk(1, 2 * ns)),
        compiler_params=_params(("parallel",)),
        name="s5prep",
    )(bb_re, bb_im, c_re.reshape(nlb, LANES, p), c_im.reshape(nlb, LANES, p), lrow, lcol)
    dtile = jnp.tile(d_skip.reshape(nlb, 1, LANES), (1, 1, tc))
    return toep, w_in, w_out, lam, dtile


def _ret_kernel(q_ref, k_ref, v_ref, g_ref, cos_ref, sin_ref, dec_ref, xi_ref,
                zeta_ref, gam_ref, o_ref, state_ref, *, heads):
    c = pl.program_id(1)

    @pl.when(c == 0)
    def _():
        state_ref[...] = jnp.zeros_like(state_ref)

    dqk = q_ref.shape[1] // heads
    dv = v_ref.shape[1] // heads
    half = dqk // 2
    cos = cos_ref[...]
    sin = sin_ref[...]

    def rot(t):
        t1, t2 = t[:, :half], t[:, half:]
        return jnp.concatenate([t1 * cos - t2 * sin, t1 * sin + t2 * cos], axis=-1)

    for h in range(heads):
        q = rot(q_ref[:, h * dqk:(h + 1) * dqk].astype(F32))
        k = rot(k_ref[:, h * dqk:(h + 1) * dqk].astype(F32)) * (dqk ** -0.5)
        v = v_ref[:, h * dv:(h + 1) * dv]
        qb = q.astype(BF16)
        scores = _dot_nt(qb, k.astype(BF16)) * dec_ref[h]
        st = state_ref[h]
        out = _dot(scores.astype(BF16), v) + _dot(qb, st.astype(BF16)) * xi_ref[h]
        kz = (k * zeta_ref[h]).astype(BF16)
        state_ref[h] = st * gam_ref[h] + _dot_tn(kz, v)
        mean = jnp.mean(out, axis=-1, keepdims=True)
        cen = out - mean
        var = jnp.mean(cen * cen, axis=-1, keepdims=True)
        normed = cen * lax.rsqrt(var + GN_EPS)
        g = g_ref[:, h * dv:(h + 1) * dv].astype(F32)
        o_ref[:, h * dv:(h + 1) * dv] = (g * jax.nn.sigmoid(g) * normed).astype(BF16)


def _retention(proj, d, batch, seq):
    t = proj.shape[0]
    heads = RET_HEADS
    ch = min(RET_KERNEL_CHUNK, seq)
    nc = seq // ch
    dqk = d // 2 // heads
    dv = d // heads
    half = dqk // 2
    inv = ROPE_BASE ** (-jnp.arange(0, dqk, 2, dtype=F32) / dqk)
    ang = jnp.arange(seq, dtype=F32)[:, None] * inv[None, :]
    cos, sin = jnp.cos(ang), jnp.sin(ang)
    log_gamma = jnp.log(1.0 - 2.0 ** (-5.0 - jnp.arange(heads, dtype=F32)))
    idx = jnp.arange(ch, dtype=F32)
    rel = idx[:, None] - idx[None, :]
    dec = jnp.where(rel[None] >= 0,
                    jnp.exp(jnp.maximum(rel, 0.0)[None] * log_gamma[:, None, None]), 0.0)
    xi = jnp.exp((idx + 1.0)[None, :] * log_gamma[:, None])[:, :, None]
    zeta = jnp.exp((ch - 1.0 - idx)[None, :] * log_gamma[:, None])[:, :, None]
    gam = jnp.exp(ch * log_gamma)[:, None, None]
    row = lambda b, c: b * nc + c
    return pl.pallas_call(
        functools.partial(_ret_kernel, heads=heads),
        out_shape=jax.ShapeDtypeStruct((t, d), BF16),
        grid=(batch, nc),
        in_specs=[
            pl.BlockSpec((ch, d // 2), lambda b, c: (row(b, c), 0)),
            pl.BlockSpec((ch, d // 2), lambda b, c: (row(b, c), 1)),
            pl.BlockSpec((ch, d), lambda b, c: (row(b, c), 1)),
            pl.BlockSpec((ch, d), lambda b, c: (row(b, c), 2)),
            pl.BlockSpec((ch, half), lambda b, c: (c, 0)),
            pl.BlockSpec((ch, half), lambda b, c: (c, 0)),
            pl.BlockSpec((heads, ch, ch), lambda b, c: (0, 0, 0)),
            pl.BlockSpec((heads, ch, 1), lambda b, c: (0, 0, 0)),
            pl.BlockSpec((heads, ch, 1), lambda b, c: (0, 0, 0)),
            pl.BlockSpec((heads, 1, 1), lambda b, c: (0, 0, 0)),
        ],
        out_specs=pl.BlockSpec((ch, d), lambda b, c: (row(b, c), 0)),
        scratch_shapes=[pltpu.VMEM((heads, dqk, dv), F32)],
        compiler_params=_params(("parallel", "arbitrary")),
        name="retention",
    )(proj, proj, proj, proj, cos, sin, dec, xi, zeta, gam)


def _resident(shape):
    return pl.BlockSpec(shape, lambda *_: (0,) * len(shape), pipeline_mode=pl.Buffered(1))


def _mix_kernel(og_ref, yg_ref, ga_ref, gb_ref, h_ref, wo_ref, wv_ref, wg_ref,
                wout_ref, o_ref, *, cn):
    d = h_ref.shape[1]
    og = og_ref[...]
    yg = yg_ref[...].astype(BF16)
    acc = h_ref[...]
    for j in range(d // cn):
        cols = slice(j * cn, (j + 1) * cn)
        y_b = _dot(og, wo_ref[:, cols])
        y_a = _dot(yg, wv_ref[:, cols]) * jax.nn.sigmoid(_dot(yg, wg_ref[:, cols]))
        merged = (jax.nn.sigmoid(ga_ref[:, cols].astype(F32)) * y_a
                  + jax.nn.sigmoid(gb_ref[:, cols].astype(F32)) * y_b)
        acc = acc + _dot(merged.astype(BF16), wout_ref[cols, :])
    o_ref[...] = acc


def _mix(og, yg, proj, h, w_o, w_v, w_g, w_out, *, tm=256, cn=512):
    t, d = h.shape
    sw = yg.shape[1]
    tm, cn = min(tm, t), min(cn, d)
    return pl.pallas_call(
        functools.partial(_mix_kernel, cn=cn),
        out_shape=jax.ShapeDtypeStruct((t, d), F32),
        grid=(t // tm,),
        in_specs=[
            pl.BlockSpec((tm, d), lambda i: (i, 0)),
            pl.BlockSpec((tm, sw), lambda i: (i, 0)),
            pl.BlockSpec((tm, d), lambda i: (i, 3)),
            pl.BlockSpec((tm, d), lambda i: (i, 4)),
            pl.BlockSpec((tm, d), lambda i: (i, 0)),
            _resident((d, d)), _resident((sw, d)), _resident((sw, d)), _resident((d, d)),
        ],
        out_specs=pl.BlockSpec((tm, d), lambda i: (i, 0)),
        compiler_params=_params(("arbitrary",)),
        name="mix",
    )(og, yg, proj, proj, h, w_o, w_v, w_g, w_out)


def _xattn_kernel(h_ref, g_ref, wq_ref, k_ref, v_ref, wo_ref, o_ref, *, heads):
    h = h_ref[...]
    d = h.shape[1]
    dh = d // heads
    xn = _rms(h, g_ref[...]).astype(BF16)
    outs = []
    for hd in range(heads):
        cols = slice(hd * dh, (hd + 1) * dh)
        q = _dot(xn, wq_ref[:, cols])
        s = _dot_nt(q.astype(BF16), k_ref[0, :, cols]) * (dh ** -0.5)
        e = jnp.exp(s - jnp.max(s, axis=-1, keepdims=True))
        p = e / jnp.sum(e, axis=-1, keepdims=True)
        outs.append(_dot(p.astype(BF16), v_ref[0, :, cols]).astype(BF16))
    o_ref[...] = h + _dot(jnp.concatenate(outs, axis=-1), wo_ref[...])


def _xattn(h, gain, wq, kmem, vmem, wo, batch, *, tm=1024):
    t, d = h.shape
    seq = t // batch
    mlen = kmem.shape[1]
    tm = min(tm, seq)
    nt = seq // tm
    return pl.pallas_call(
        functools.partial(_xattn_kernel, heads=XATTN_HEADS),
        out_shape=jax.ShapeDtypeStruct((t, d), F32),
        grid=(batch, nt),
        in_specs=[
            pl.BlockSpec((tm, d), lambda b, i: (b * nt + i, 0)),
            _resident((1, d)),
            _resident((d, d)),
            pl.BlockSpec((1, mlen, d), lambda b, i: (b, 0, 0)),
            pl.BlockSpec((1, mlen, d), lambda b, i: (b, 0, 0)),
            _resident((d, d)),
        ],
        out_specs=pl.BlockSpec((tm, d), lambda b, i: (b * nt + i, 0)),
        compiler_params=_params(("arbitrary", "arbitrary")),
        name="xattn",
    )(h, gain.reshape(1, d), wq, kmem, vmem, wo)


def kernel(x, mem, ffn1_norm, ffn1_w1, ffn1_w3, ffn1_w2, mix_norm, w_in, s5_a_re, s5_a_im, s5_log_dt, s5_b_re, s5_b_im, s5_c_re, s5_c_im, s5_d, s5_glu_v, s5_glu_g, ret_w_o, w_out, xattn_norm, mem_norm, xattn_wq, xattn_wk, xattn_wv, xattn_wo, ffn2_norm, ffn2_w1, ffn2_w3, ffn2_w2, final_norm):
    batch, seq, d = x.shape
    mlen = mem.shape[1]
    depth = ffn1_w1.shape[0]
    t = batch * seq
    s5_width = s5_d.shape[1]
    bf = lambda w: w.astype(BF16)

    h = x.reshape(t, d)
    mem2 = mem.reshape(batch * mlen, d)
    for l in range(depth):
        last = l == depth - 1
        h = _ffn(h, ffn1_norm[l], bf(ffn1_w1[l]), bf(ffn1_w3[l]), bf(ffn1_w2[l]))
        proj, us5 = _inproj(h, mix_norm[l], bf(w_in[l]), s5_width)
        toep, s_in, s_out, lam, dtile = _s5_weights(
            s5_a_re[l], s5_a_im[l], s5_log_dt[l], s5_b_re[l], s5_b_im[l],
            s5_c_re[l], s5_c_im[l], s5_d[l])
        yg = _s5(us5, toep, s_in, s_out, lam, dtile, batch)
        og = _retention(proj, d, batch, seq)
        h = _mix(og, yg, proj, h, bf(ret_w_o[l]), bf(s5_glu_v[l]), bf(s5_glu_g[l]),
                 bf(w_out[l]))
        kmem = _normmm(mem2, mem_norm[l], bf(xattn_wk[l])).reshape(batch, mlen, d)
        vmem = _normmm(mem2, mem_norm[l], bf(xattn_wv[l])).reshape(batch, mlen, d)
        h = _xattn(h, xattn_norm[l], bf(xattn_wq[l]), kmem, vmem, bf(xattn_wo[l]), batch)
        h = _ffn(h, ffn2_norm[l], bf(ffn2_w1[l]), bf(ffn2_w3[l]), bf(ffn2_w2[l]),
                 final_norm if last else None)
    if depth == 0:
        raise ValueError("depth must be >= 1")
    return h.reshape(batch, seq, d)
```

```python
import functools

import jax
import jax.numpy as jnp
from jax import lax
from jax.experimental import pallas as pl
from jax.experimental.pallas import tpu as pltpu

F32 = jnp.float32
BF16 = jnp.bfloat16

RMS_EPS = 1e-6
GN_EPS = 1e-5
ROPE_BASE = 10000.0
S5_GROUP = 16
RET_HEADS = 4
XATTN_HEADS = 4
RET_KERNEL_CHUNK = 256
S5_CHUNK = 16
LANES = 128
GROUPS_PER_LANE_BLOCK = LANES // S5_GROUP
VMEM_LIMIT_BYTES = 60 * 1024 * 1024
MXU_TILE = 256


def _params(semantics):
    return pltpu.CompilerParams(dimension_semantics=semantics,
                                vmem_limit_bytes=VMEM_LIMIT_BYTES)


def _rms(x, gain):
    ms = jnp.mean(x * x, axis=-1, keepdims=True)
    return x * lax.rsqrt(ms + RMS_EPS) * gain


def _dot(a, b):
    return jnp.dot(a, b, preferred_element_type=F32)


def _dot_nt(a, b):
    return lax.dot_general(a, b, (((1,), (1,)), ((), ())), preferred_element_type=F32)


def _dot_tn(a, b):
    return lax.dot_general(a, b, (((0,), (0,)), ((), ())), preferred_element_type=F32)


def _run_casts(cast_in, cast_out):
    for src_ref, dst_ref in zip(cast_in, cast_out):
        dst_ref[...] = src_ref[...].astype(BF16)


def _ffn_kernel(*refs, final, ncast):
    nin = 6 if final else 5
    x_ref, g_ref, w1_ref, w3_ref, w2_ref = refs[:5]
    fg_ref = refs[5] if final else None
    cast_in = refs[nin:nin + ncast]
    o_ref = refs[nin + ncast]
    cast_out = refs[nin + ncast + 1:nin + 2 * ncast + 1]
    xn_ref = refs[-1]
    j = pl.program_id(1)

    _run_casts(cast_in, cast_out)

    @pl.when(j == 0)
    def _():
        x = x_ref[...]
        xn_ref[...] = _rms(x, g_ref[...]).astype(BF16)
        o_ref[...] = x

    xn = xn_ref[...]
    a = _dot(xn, w1_ref[...])
    b = _dot(xn, w3_ref[...])
    mid = (a * jax.nn.sigmoid(a) * b * 0.5).astype(BF16)
    o_ref[...] += _dot(mid, w2_ref[...])

    if final:
        @pl.when(j == pl.num_programs(1) - 1)
        def _():
            o_ref[...] = _rms(o_ref[...], fg_ref[...])


def _cast_block(shape, nt, nf):
    r, c = shape

    def tiled(rows, cols):
        return rows % 16 == 0 and cols % LANES == 0

    if r % nt == 0 and c % nf == 0 and tiled(r // nt, c // nf):
        return (r // nt, c // nf), lambda i, j: (i, j)
    if r % nf == 0 and c % nt == 0 and tiled(r // nf, c // nt):
        return (r // nf, c // nt), lambda i, j: (j, i)
    assert r % nt == 0 and (r // nt) % 16 == 0, shape
    ncol = max(n for n in range(1, nf + 1) if c % n == 0 and (c // n) % LANES == 0)
    return (r // nt, c // ncol), lambda i, j: (i, jnp.minimum(j, ncol - 1))


def _ffn(x, gain, w1, w3, w2, final_gain=None, casts=(), *, tm=1024, tf=512):
    t, d = x.shape
    f = w1.shape[1]
    tm, tf = min(tm, t), min(tf, f)
    final = final_gain is not None
    in_specs = [
        pl.BlockSpec((tm, d), lambda i, j: (i, 0)),
        pl.BlockSpec((1, d), lambda i, j: (0, 0)),
        pl.BlockSpec((d, tf), lambda i, j: (0, j)),
        pl.BlockSpec((d, tf), lambda i, j: (0, j)),
        pl.BlockSpec((tf, d), lambda i, j: (j, 0)),
    ]
    args = [x, gain.reshape(1, d), w1, w3, w2]
    if final:
        in_specs.append(pl.BlockSpec((1, d), lambda i, j: (0, 0)))
        args.append(final_gain.reshape(1, d))
    cast_specs = [pl.BlockSpec(*_cast_block(w.shape, t // tm, f // tf)) for w in casts]
    outs = pl.pallas_call(
        functools.partial(_ffn_kernel, final=final, ncast=len(casts)),
        out_shape=[jax.ShapeDtypeStruct((t, d), F32)]
        + [jax.ShapeDtypeStruct(w.shape, BF16) for w in casts],
        grid=(t // tm, f // tf),
        in_specs=in_specs + cast_specs,
        out_specs=[pl.BlockSpec((tm, d), lambda i, j: (i, 0))] + cast_specs,
        scratch_shapes=[pltpu.VMEM((tm, d), BF16)],
        compiler_params=_params(("arbitrary", "arbitrary")),
        name="ffn_final" if final else "ffn",
    )(*args, *casts)
    return outs


def _inproj_kernel(*refs, ncast):
    x_ref, g_ref, w_ref = refs[:3]
    cast_in = refs[3:3 + ncast]
    proj_ref, us5_ref = refs[3 + ncast:5 + ncast]
    cast_out = refs[5 + ncast:5 + 2 * ncast]
    xn_ref = refs[-1]
    j = pl.program_id(1)
    _run_casts(cast_in, cast_out)

    @pl.when(j == 0)
    def _():
        xn_ref[...] = _rms(x_ref[...], g_ref[...]).astype(BF16)

    res = _dot(xn_ref[...], w_ref[...])

    @pl.when(j == 0)
    def _():
        us5_ref[...] = res

    @pl.when(j > 0)
    def _():
        proj_ref[...] = res.astype(BF16)


def _inproj(x, gain, w, s5_width, casts=(), *, tm=1024):
    t, d = x.shape
    n = w.shape[1]
    tn = s5_width
    tm = min(tm, t)
    cast_specs = [pl.BlockSpec(*_cast_block(c.shape, t // tm, n // tn)) for c in casts]
    return pl.pallas_call(
        functools.partial(_inproj_kernel, ncast=len(casts)),
        out_shape=[jax.ShapeDtypeStruct((t, n - tn), BF16),
                   jax.ShapeDtypeStruct((t, tn), F32)]
        + [jax.ShapeDtypeStruct(c.shape, BF16) for c in casts],
        grid=(t // tm, n // tn),
        in_specs=[
            pl.BlockSpec((tm, d), lambda i, j: (i, 0)),
            pl.BlockSpec((1, d), lambda i, j: (0, 0)),
            pl.BlockSpec((d, tn), lambda i, j: (0, j)),
        ] + cast_specs,
        out_specs=[
            pl.BlockSpec((tm, tn), lambda i, j: (i, jnp.maximum(j - 1, 0))),
            pl.BlockSpec((tm, tn), lambda i, j: (i, 0)),
        ] + cast_specs,
        scratch_shapes=[pltpu.VMEM((tm, d), BF16)],
        compiler_params=_params(("arbitrary", "arbitrary")),
        name="inproj",
    )(x, gain.reshape(1, d), w, *casts)


def _normmm_kernel(x_ref, g_ref, w_ref, o_ref):
    xn = _rms(x_ref[...], g_ref[...]).astype(BF16)
    o_ref[...] = _dot(xn, w_ref[...]).astype(BF16)


def _normmm(x, gain, w, *, tn=512):
    t, d = x.shape
    n = w.shape[1]
    tn = min(tn, n)
    return pl.pallas_call(
        _normmm_kernel,
        out_shape=jax.ShapeDtypeStruct((t, n), BF16),
        grid=(n // tn,),
        in_specs=[
            pl.BlockSpec((t, d), lambda j: (0, 0)),
            pl.BlockSpec((1, d), lambda j: (0, 0)),
            pl.BlockSpec((d, tn), lambda j: (0, j)),
        ],
        out_specs=pl.BlockSpec((t, tn), lambda j: (0, j)),
        compiler_params=_params(("parallel",)),
        name="normmm",
    )(x, gain.reshape(1, d), w)


def _s5_kernel(u_ref, m_ref, win_ref, wout_ref, lam_ref, d_ref, y_ref, z_ref, s_ref, *, tc, nb):
    rb = z_ref.shape[0] // nb
    seq = rb * tc
    u32 = jnp.concatenate(
        [jnp.concatenate([u_ref[pl.ds(b * seq + j, rb, stride=tc), :] for j in range(tc)], axis=-1)
         for b in range(nb)], axis=0)
    u = u32.astype(BF16)
    z_ref[...] = _dot(u, win_ref[0])
    ns = z_ref.shape[1] // 2
    lam_re = lam_ref[0, :, :ns]
    lam_im = lam_ref[0, :, ns:]

    def step(c, carry):
        out = []
        for b in range(nb):
            s_re, s_im = carry[2 * b], carry[2 * b + 1]
            row = pl.ds(b * rb + c, 1)
            s_ref[row, :ns] = s_re
            s_ref[row, ns:] = s_im
            out += [lam_re * s_re - lam_im * s_im + z_ref[row, :ns],
                    lam_re * s_im + lam_im * s_re + z_ref[row, ns:]]
        return tuple(out)

    zero = jnp.zeros((1, ns), F32)
    lax.fori_loop(0, rb, step, (zero,) * (2 * nb))

    s = s_ref[...].astype(BF16)
    per_tile = MXU_TILE // LANES
    for cb in range(tc // per_tile):
        cols = slice(cb * MXU_TILE, (cb + 1) * MXU_TILE)
        kmax = (cb + 1) * MXU_TILE
        y = (_dot(u[:, :kmax], m_ref[0, :kmax, cols]) + _dot(s, wout_ref[0, :, cols])
             + d_ref[0, :, cols] * u32[:, cols])
        y = jax.nn.gelu(y)
        for b in range(nb):
            for i in range(per_tile):
                y_ref[pl.ds(b * seq + cb * per_tile + i, rb, stride=tc), :] = (
                    y[b * rb:(b + 1) * rb, i * LANES:(i + 1) * LANES])


def _s5(u, m, w_in, w_out, lam, dtile, batch, *, nb=1):
    t, sw = u.shape
    seq = t // batch
    tc = S5_CHUNK
    nlb, width, nstate = w_in.shape
    nb = nb if batch % nb == 0 else 1
    rows = nb * seq // tc
    return pl.pallas_call(
        functools.partial(_s5_kernel, tc=tc, nb=nb),
        out_shape=jax.ShapeDtypeStruct((t, sw), F32),
        grid=(nlb, batch // nb),
        in_specs=[
            pl.BlockSpec((nb * seq, LANES), lambda k, b: (b, k)),
            pl.BlockSpec((1, width, width), lambda k, b: (k, 0, 0)),
            pl.BlockSpec((1, width, nstate), lambda k, b: (k, 0, 0)),
            pl.BlockSpec((1, nstate, width), lambda k, b: (k, 0, 0)),
            pl.BlockSpec((1, 1, nstate), lambda k, b: (k, 0, 0)),
            pl.BlockSpec((1, 1, width), lambda k, b: (k, 0, 0)),
        ],
        out_specs=pl.BlockSpec((nb * seq, LANES), lambda k, b: (b, k)),
        scratch_shapes=[pltpu.VMEM((rows, nstate), F32), pltpu.VMEM((rows, nstate), F32)],
        compiler_params=_params(("parallel", "arbitrary")),
        name="s5",
    )(u, m, w_in, w_out, lam, dtile)


def _log2(n):
    assert n & (n - 1) == 0, n
    return n.bit_length() - 1


def _s5prep_kernel(bre_ref, bim_ref, cre_ref, cim_ref, lrow_ref, lcol_ref,
                   toep_ref, win_ref, wout_ref, lam_ref, *, tc):
    hi = lax.Precision.HIGHEST
    ns, gi = bre_ref.shape[1], bre_ref.shape[2]
    lanes, p = cre_ref.shape[1], cre_ref.shape[2]

    def iota(shape, dim):
        return lax.broadcasted_iota(jnp.int32, shape, dim)

    def grp(x, size):
        return lax.shift_right_logical(x, _log2(size))

    e_i = ((iota((gi, lanes), 1) & (gi - 1)) == iota((gi, lanes), 0)).astype(F32)
    e_p = ((iota((p, ns), 1) & (p - 1)) == iota((p, ns), 0)).astype(F32)
    mask_b = grp(iota((ns, lanes), 0), p) == grp(iota((ns, lanes), 1), gi)
    mask_c = grp(iota((lanes, ns), 0), gi) == grp(iota((lanes, ns), 1), p)

    def bdiag_b(ref):
        return jnp.where(mask_b, jnp.dot(ref[0], e_i, precision=hi,
                                         preferred_element_type=F32), 0.0).T

    def bdiag_c(ref):
        return jnp.where(mask_c, jnp.dot(ref[0], e_p, precision=hi,
                                         preferred_element_type=F32), 0.0).T

    b_re, b_im = bdiag_b(bre_ref), bdiag_b(bim_ref)
    c_re, c_im = bdiag_c(cre_ref), bdiag_c(cim_ref)
    l_re, l_im = lrow_ref[0, 0:1, :], lrow_ref[0, 1:2, :]
    lc_re, lc_im = lcol_ref[0, :, 0:1], lcol_ref[0, :, 1:2]
    p_re, p_im = jnp.ones_like(l_re), jnp.zeros_like(l_im)
    q_re, q_im = jnp.ones_like(lc_re), jnp.zeros_like(lc_im)
    zero_tile = jnp.zeros((lanes, lanes), BF16)

    def tile(i):
        return slice(i * lanes, (i + 1) * lanes)

    for n in range(tc):
        lb_re = b_re * p_re - b_im * p_im
        lb_im = b_re * p_im + b_im * p_re
        win_ref[0, tile(tc - 1 - n), :ns] = lb_re.astype(BF16)
        win_ref[0, tile(tc - 1 - n), ns:] = lb_im.astype(BF16)
        kern = (jnp.dot(lb_re, c_re, precision=hi, preferred_element_type=F32)
                - jnp.dot(lb_im, c_im, precision=hi, preferred_element_type=F32)).astype(BF16)
        for j in range(tc - n):
            toep_ref[0, tile(j), tile(j + n)] = kern
        if n:
            for t in range(tc - n):
                toep_ref[0, tile(t + n), tile(t)] = zero_tile
        p_re, p_im = p_re * l_re - p_im * l_im, p_re * l_im + p_im * l_re
        q_re, q_im = q_re * lc_re - q_im * lc_im, q_re * lc_im + q_im * lc_re
        wout_ref[0, :ns, tile(n)] = (c_re * q_re - c_im * q_im).astype(BF16)
        wout_ref[0, ns:, tile(n)] = (-(c_re * q_im + c_im * q_re)).astype(BF16)
    lam_ref[0, :, :ns] = p_re
    lam_ref[0, :, ns:] = p_im


def _s5_weights(a_re, a_im, log_dt, b_re, b_im, c_re, c_im, d_skip):
    tc = S5_CHUNK
    g, p = a_re.shape
    gl = GROUPS_PER_LANE_BLOCK
    nlb = g // gl
    ns = gl * p
    dt = jnp.exp(log_dt)[:, None]
    mag = jnp.exp(a_re * dt)
    l_re = mag * jnp.cos(a_im * dt)
    l_im = mag * jnp.sin(a_im * dt)
    den = a_re * a_re + a_im * a_im
    n_re = l_re - 1.0
    n_im = l_im
    f_re = (n_re * a_re + n_im * a_im) / den
    f_im = (n_im * a_re - n_re * a_im) / den
    bb_re = (f_re[..., None] * b_re - f_im[..., None] * b_im).reshape(nlb, ns, S5_GROUP)
    bb_im = (f_re[..., None] * b_im + f_im[..., None] * b_re).reshape(nlb, ns, S5_GROUP)
    lrow = jnp.stack([l_re.reshape(nlb, ns), l_im.reshape(nlb, ns)], axis=1)
    lcol = jnp.stack([l_re.reshape(nlb, ns), l_im.reshape(nlb, ns)], axis=2)
    width = tc * LANES
    blk = lambda *shape: pl.BlockSpec((1,) + shape, lambda k: (k, 0, 0))
    toep, w_in, w_out, lam = pl.pallas_call(
        functools.partial(_s5prep_kernel, tc=tc),
        out_shape=(jax.ShapeDtypeStruct((nlb, width, width), BF16),
                   jax.ShapeDtypeStruct((nlb, width, 2 * ns), BF16),
                   jax.ShapeDtypeStruct((nlb, 2 * ns, width), BF16),
                   jax.ShapeDtypeStruct((nlb, 1, 2 * ns), F32)),
        grid=(nlb,),
        in_specs=[blk(ns, S5_GROUP), blk(ns, S5_GROUP), blk(LANES, p), blk(LANES, p),
                  blk(2, ns), blk(ns, 2)],
        out_specs=(blk(width, width), blk(width, 2 * ns), blk(2 * ns, width),
                   blk(1, 2 * ns)),
        compiler_params=_params(("parallel",)),
        name="s5prep",
    )(bb_re, bb_im, c_re.reshape(nlb, LANES, p), c_im.reshape(nlb, LANES, p), lrow, lcol)
    dtile = jnp.tile(d_skip.reshape(nlb, 1, LANES), (1, 1, tc))
    return toep, w_in, w_out, lam, dtile


def _ret_kernel(q_ref, k_ref, v_ref, g_ref, cos_ref, sin_ref, dec_ref, xi_ref,
                zeta_ref, gam_ref, o_ref, state_ref, *, heads):
    c = pl.program_id(1)

    @pl.when(c == 0)
    def _():
        state_ref[...] = jnp.zeros_like(state_ref)

    dqk = q_ref.shape[1] // heads
    dv = v_ref.shape[1] // heads
    half = dqk // 2
    cos = cos_ref[...]
    sin = sin_ref[...]

    def rot(t):
        t1, t2 = t[:, :half], t[:, half:]
        return jnp.concatenate([t1 * cos - t2 * sin, t1 * sin + t2 * cos], axis=-1)

    for h in range(heads):
        q = rot(q_ref[:, h * dqk:(h + 1) * dqk].astype(F32))
        k = rot(k_ref[:, h * dqk:(h + 1) * dqk].astype(F32)) * (dqk ** -0.5)
        v = v_ref[:, h * dv:(h + 1) * dv]
        qb = q.astype(BF16)
        scores = _dot_nt(qb, k.astype(BF16)) * dec_ref[h]
        st = state_ref[h]
        out = _dot(scores.astype(BF16), v) + _dot(qb, st.astype(BF16)) * xi_ref[h]
        kz = (k * zeta_ref[h]).astype(BF16)
        state_ref[h] = st * gam_ref[h] + _dot_tn(kz, v)
        mean = jnp.mean(out, axis=-1, keepdims=True)
        cen = out - mean
        var = jnp.mean(cen * cen, axis=-1, keepdims=True)
        normed = cen * lax.rsqrt(var + GN_EPS)
        g = g_ref[:, h * dv:(h + 1) * dv].astype(F32)
        o_ref[:, h * dv:(h + 1) * dv] = (g * jax.nn.sigmoid(g) * normed).astype(BF16)


def _retention(proj, d, batch, seq):
    t = proj.shape[0]
    heads = RET_HEADS
    ch = min(RET_KERNEL_CHUNK, seq)
    nc = seq // ch
    dqk = d // 2 // heads
    dv = d // heads
    half = dqk // 2
    inv = ROPE_BASE ** (-jnp.arange(0, dqk, 2, dtype=F32) / dqk)
    ang = jnp.arange(seq, dtype=F32)[:, None] * inv[None, :]
    cos, sin = jnp.cos(ang), jnp.sin(ang)
    log_gamma = jnp.log(1.0 - 2.0 ** (-5.0 - jnp.arange(heads, dtype=F32)))
    idx = jnp.arange(ch, dtype=F32)
    rel = idx[:, None] - idx[None, :]
    dec = jnp.where(rel[None] >= 0,
                    jnp.exp(jnp.maximum(rel, 0.0)[None] * log_gamma[:, None, None]), 0.0)
    xi = jnp.exp((idx + 1.0)[None, :] * log_gamma[:, None])[:, :, None]
    zeta = jnp.exp((ch - 1.0 - idx)[None, :] * log_gamma[:, None])[:, :, None]
    gam = jnp.exp(ch * log_gamma)[:, None, None]
    row = lambda b, c: b * nc + c
    return pl.pallas_call(
        functools.partial(_ret_kernel, heads=heads),
        out_shape=jax.ShapeDtypeStruct((t, d), BF16),
        grid=(batch, nc),
        in_specs=[
            pl.BlockSpec((ch, d // 2), lambda b, c: (row(b, c), 0)),
            pl.BlockSpec((ch, d // 2), lambda b, c: (row(b, c), 1)),
            pl.BlockSpec((ch, d), lambda b, c: (row(b, c), 1)),
            pl.BlockSpec((ch, d), lambda b, c: (row(b, c), 2)),
            pl.BlockSpec((ch, half), lambda b, c: (c, 0)),
            pl.BlockSpec((ch, half), lambda b, c: (c, 0)),
            pl.BlockSpec((heads, ch, ch), lambda b, c: (0, 0, 0)),
            pl.BlockSpec((heads, ch, 1), lambda b, c: (0, 0, 0)),
            pl.BlockSpec((heads, ch, 1), lambda b, c: (0, 0, 0)),
            pl.BlockSpec((heads, 1, 1), lambda b, c: (0, 0, 0)),
        ],
        out_specs=pl.BlockSpec((ch, d), lambda b, c: (row(b, c), 0)),
        scratch_shapes=[pltpu.VMEM((heads, dqk, dv), F32)],
        compiler_params=_params(("parallel", "arbitrary")),
        name="retention",
    )(proj, proj, proj, proj, cos, sin, dec, xi, zeta, gam)


def _resident(shape):
    return pl.BlockSpec(shape, lambda *_: (0,) * len(shape), pipeline_mode=pl.Buffered(1))


def _mix_kernel(og_ref, yg_ref, ga_ref, gb_ref, h_ref, wo_ref, wv_ref, wg_ref,
                wout_ref, o_ref, *, cn):
    d = h_ref.shape[1]
    og = og_ref[...]
    yg = yg_ref[...].astype(BF16)
    acc = h_ref[...]
    for j in range(d // cn):
        cols = slice(j * cn, (j + 1) * cn)
        y_b = _dot(og, wo_ref[:, cols])
        y_a = _dot(yg, wv_ref[:, cols]) * jax.nn.sigmoid(_dot(yg, wg_ref[:, cols]))
        merged = (jax.nn.sigmoid(ga_ref[:, cols].astype(F32)) * y_a
                  + jax.nn.sigmoid(gb_ref[:, cols].astype(F32)) * y_b)
        acc = acc + _dot(merged.astype(BF16), wout_ref[cols, :])
    o_ref[...] = acc


def _mix(og, yg, proj, h, w_o, w_v, w_g, w_out, *, tm=256, cn=512):
    t, d = h.shape
    sw = yg.shape[1]
    tm, cn = min(tm, t), min(cn, d)
    return pl.pallas_call(
        functools.partial(_mix_kernel, cn=cn),
        out_shape=jax.ShapeDtypeStruct((t, d), F32),
        grid=(t // tm,),
        in_specs=[
            pl.BlockSpec((tm, d), lambda i: (i, 0)),
            pl.BlockSpec((tm, sw), lambda i: (i, 0)),
            pl.BlockSpec((tm, d), lambda i: (i, 3)),
            pl.BlockSpec((tm, d), lambda i: (i, 4)),
            pl.BlockSpec((tm, d), lambda i: (i, 0)),
            _resident((d, d)), _resident((sw, d)), _resident((sw, d)), _resident((d, d)),
        ],
        out_specs=pl.BlockSpec((tm, d), lambda i: (i, 0)),
        compiler_params=_params(("arbitrary",)),
        name="mix",
    )(og, yg, proj, proj, h, w_o, w_v, w_g, w_out)


def _xattn_kernel(h_ref, g_ref, wq_ref, k_ref, v_ref, wo_ref, o_ref, *, heads):
    h = h_ref[...]
    d = h.shape[1]
    dh = d // heads
    xn = _rms(h, g_ref[...]).astype(BF16)
    outs = []
    for hd in range(heads):
        cols = slice(hd * dh, (hd + 1) * dh)
        q = _dot(xn, wq_ref[:, cols])
        s = _dot_nt(q.astype(BF16), k_ref[0, :, cols]) * (dh ** -0.5)
        e = jnp.exp(s - jnp.max(s, axis=-1, keepdims=True))
        p = e / jnp.sum(e, axis=-1, keepdims=True)
        outs.append(_dot(p.astype(BF16), v_ref[0, :, cols]).astype(BF16))
    o_ref[...] = h + _dot(jnp.concatenate(outs, axis=-1), wo_ref[...])


def _xattn(h, gain, wq, kmem, vmem, wo, batch, *, tm=1024):
    t, d = h.shape
    seq = t // batch
    mlen = kmem.shape[1]
    tm = min(tm, seq)
    nt = seq // tm
    return pl.pallas_call(
        functools.partial(_xattn_kernel, heads=XATTN_HEADS),
        out_shape=jax.ShapeDtypeStruct((t, d), F32),
        grid=(batch, nt),
        in_specs=[
            pl.BlockSpec((tm, d), lambda b, i: (b * nt + i, 0)),
            _resident((1, d)),
            _resident((d, d)),
            pl.BlockSpec((1, mlen, d), lambda b, i: (b, 0, 0)),
            pl.BlockSpec((1, mlen, d), lambda b, i: (b, 0, 0)),
            _resident((d, d)),
        ],
        out_specs=pl.BlockSpec((tm, d), lambda b, i: (b * nt + i, 0)),
        compiler_params=_params(("arbitrary", "arbitrary")),
        name="xattn",
    )(h, gain.reshape(1, d), wq, kmem, vmem, wo)


def kernel(x, mem, ffn1_norm, ffn1_w1, ffn1_w3, ffn1_w2, mix_norm, w_in, s5_a_re, s5_a_im, s5_log_dt, s5_b_re, s5_b_im, s5_c_re, s5_c_im, s5_d, s5_glu_v, s5_glu_g, ret_w_o, w_out, xattn_norm, mem_norm, xattn_wq, xattn_wk, xattn_wv, xattn_wo, ffn2_norm, ffn2_w1, ffn2_w3, ffn2_w2, final_norm):
    batch, seq, d = x.shape
    mlen = mem.shape[1]
    depth = ffn1_w1.shape[0]
    t = batch * seq
    s5_width = s5_d.shape[1]
    bf = lambda w: w.astype(BF16)

    h = x.reshape(t, d)
    mem2 = mem.reshape(batch * mlen, d)
    for l in range(depth):
        last = l == depth - 1
        later = (w_in, s5_glu_v, s5_glu_g, ret_w_o, w_out, xattn_wq, xattn_wk, xattn_wv, xattn_wo)
        h, c_in, c_gv, c_gg, c_ro, c_out, c_wq, c_wk, c_wv, c_wo = _ffn(
            h, ffn1_norm[l], bf(ffn1_w1[l]), bf(ffn1_w3[l]), bf(ffn1_w2[l]),
            casts=[w[l] for w in later])
        proj, us5, c_f1, c_f3, c_f2 = _inproj(
            h, mix_norm[l], c_in, s5_width, casts=[ffn2_w1[l], ffn2_w3[l], ffn2_w2[l]])
        toep, s_in, s_out, lam, dtile = _s5_weights(
            s5_a_re[l], s5_a_im[l], s5_log_dt[l], s5_b_re[l], s5_b_im[l],
            s5_c_re[l], s5_c_im[l], s5_d[l])
        yg = _s5(us5, toep, s_in, s_out, lam, dtile, batch)
        og = _retention(proj, d, batch, seq)
        h = _mix(og, yg, proj, h, c_ro, c_gv, c_gg, c_out)
        kmem = _normmm(mem2, mem_norm[l], c_wk).reshape(batch, mlen, d)
        vmem = _normmm(mem2, mem_norm[l], c_wv).reshape(batch, mlen, d)
        h = _xattn(h, xattn_norm[l], c_wq, kmem, vmem, c_wo, batch)
        h, = _ffn(h, ffn2_norm[l], c_f1, c_f3, c_f2, final_norm if last else None)
    if depth == 0:
        raise ValueError("depth must be >= 1")
    return h.reshape(batch, seq, d)
```

```python
import functools

import jax
import jax.numpy as jnp
from jax import lax
from jax.experimental import pallas as pl
from jax.experimental.pallas import tpu as pltpu

F32 = jnp.float32
BF16 = jnp.bfloat16

RMS_EPS = 1e-6
GN_EPS = 1e-5
ROPE_BASE = 10000.0
S5_GROUP = 16
RET_HEADS = 4
XATTN_HEADS = 4
RET_KERNEL_CHUNK = 256
S5_CHUNK = 16
LANES = 128
GROUPS_PER_LANE_BLOCK = LANES // S5_GROUP
VMEM_LIMIT_BYTES = 60 * 1024 * 1024
MXU_TILE = 256


def _params(semantics):
    return pltpu.CompilerParams(dimension_semantics=semantics,
                                vmem_limit_bytes=VMEM_LIMIT_BYTES)


def _resident(shape):
    return pl.BlockSpec(shape, lambda *_: (0,) * len(shape), pipeline_mode=pl.Buffered(1))


def _rms(x, gain):
    ms = jnp.mean(x * x, axis=-1, keepdims=True)
    return x * lax.rsqrt(ms + RMS_EPS) * gain


def _dot(a, b):
    return jnp.dot(a, b, preferred_element_type=F32)


def _dot_nt(a, b):
    return lax.dot_general(a, b, (((1,), (1,)), ((), ())), preferred_element_type=F32)


def _dot_tn(a, b):
    return lax.dot_general(a, b, (((0,), (0,)), ((), ())), preferred_element_type=F32)


def _run_casts(cast_in, cast_out):
    for src_ref, dst_ref in zip(cast_in, cast_out):
        dst_ref[...] = src_ref[...].astype(BF16)


def _ffn_kernel(*refs, final, ncast):
    nin = 6 if final else 5
    x_ref, g_ref, w1_ref, w3_ref, w2_ref = refs[:5]
    fg_ref = refs[5] if final else None
    cast_in = refs[nin:nin + ncast]
    o_ref = refs[nin + ncast]
    cast_out = refs[nin + ncast + 1:nin + 2 * ncast + 1]
    xn_ref = refs[-1]
    j = pl.program_id(1)

    _run_casts(cast_in, cast_out)

    @pl.when(j == 0)
    def _():
        x = x_ref[...]
        xn_ref[...] = _rms(x, g_ref[...]).astype(BF16)
        o_ref[...] = x

    xn = xn_ref[...]
    a = _dot(xn, w1_ref[...])
    b = _dot(xn, w3_ref[...])
    mid = (a * jax.nn.sigmoid(a) * b * 0.5).astype(BF16)
    o_ref[...] += _dot(mid, w2_ref[...])

    if final:
        @pl.when(j == pl.num_programs(1) - 1)
        def _():
            o_ref[...] = _rms(o_ref[...], fg_ref[...])


def _first_step_only(nt):
    return lambda i, j: (jnp.minimum(i + jnp.minimum(j, 1), nt - 1), 0)


def _cast_block(shape, nt, nf):
    r, c = shape

    def tiled(rows, cols):
        return rows % 16 == 0 and cols % LANES == 0

    if r % nt == 0 and c % nf == 0 and tiled(r // nt, c // nf):
        return (r // nt, c // nf), lambda i, j: (i, j)
    if r % nf == 0 and c % nt == 0 and tiled(r // nf, c // nt):
        return (r // nf, c // nt), lambda i, j: (j, i)
    nrow = max(n for n in range(1, nt + 1) if r % n == 0 and (r // n) % 16 == 0)
    ncol = max(n for n in range(1, nf + 1) if c % n == 0 and (c // n) % LANES == 0)
    return ((r // nrow, c // ncol),
            lambda i, j: (jnp.minimum(i, nrow - 1), jnp.minimum(j, ncol - 1)))


def _ffn(x, gain, w1, w3, w2, final_gain=None, casts=(), *, tm=1024, tf=512):
    t, d = x.shape
    f = w1.shape[1]
    tm, tf = min(tm, t), min(tf, f)
    final = final_gain is not None
    in_specs = [
        pl.BlockSpec((tm, d), _first_step_only(t // tm)),
        pl.BlockSpec((1, d), lambda i, j: (0, 0)),
        pl.BlockSpec((d, tf), lambda i, j: (0, j)),
        pl.BlockSpec((d, tf), lambda i, j: (0, j)),
        pl.BlockSpec((tf, d), lambda i, j: (j, 0)),
    ]
    args = [x, gain.reshape(1, d), w1, w3, w2]
    if final:
        in_specs.append(pl.BlockSpec((1, d), lambda i, j: (0, 0)))
        args.append(final_gain.reshape(1, d))
    cast_specs = [pl.BlockSpec(*_cast_block(w.shape, t // tm, f // tf)) for w in casts]
    outs = pl.pallas_call(
        functools.partial(_ffn_kernel, final=final, ncast=len(casts)),
        out_shape=[jax.ShapeDtypeStruct((t, d), F32)]
        + [jax.ShapeDtypeStruct(w.shape, BF16) for w in casts],
        grid=(t // tm, f // tf),
        in_specs=in_specs + cast_specs,
        out_specs=[pl.BlockSpec((tm, d), lambda i, j: (i, 0))] + cast_specs,
        scratch_shapes=[pltpu.VMEM((tm, d), BF16)],
        compiler_params=_params(("arbitrary", "arbitrary")),
        name="ffn_final" if final else "ffn",
    )(*args, *casts)
    return outs


def _inproj_kernel(x_ref, g_ref, wa_ref, wb_ref, proj_ref, us5_ref, xn_ref):
    j = pl.program_id(1)
    tn = wa_ref.shape[1]

    @pl.when(j == 0)
    def _():
        xn = _rms(x_ref[...], g_ref[...]).astype(BF16)
        xn_ref[...] = xn
        us5_ref[...] = _dot(xn, wa_ref[...])

    @pl.when(j > 0)
    def _():
        xn = xn_ref[...]
        proj_ref[:, :tn] = _dot(xn, wa_ref[...]).astype(BF16)
        proj_ref[:, tn:] = _dot(xn, wb_ref[...]).astype(BF16)


def _inproj(x, gain, w, s5_width, *, tm=1024):
    t, d = x.shape
    n = w.shape[1]
    tn = s5_width
    tm = min(tm, t)
    pairs, odd = divmod(n // tn - 1, 2)
    assert odd == 0, (n, tn)
    nt, nf = t // tm, 1 + pairs
    return pl.pallas_call(
        _inproj_kernel,
        out_shape=[jax.ShapeDtypeStruct((t, n - tn), BF16),
                   jax.ShapeDtypeStruct((t, tn), F32)],
        grid=(nt, nf),
        in_specs=[
            pl.BlockSpec((tm, d), _first_step_only(nt)),
            pl.BlockSpec((1, d), lambda i, j: (0, 0)),
            pl.BlockSpec((d, tn), lambda i, j: (0, jnp.maximum(2 * j - 1, 0))),
            pl.BlockSpec((d, tn), lambda i, j: (0, 2 * jnp.maximum(j, 1))),
        ],
        out_specs=[
            pl.BlockSpec((tm, 2 * tn), lambda i, j: (i, jnp.maximum(j - 1, 0))),
            pl.BlockSpec((tm, tn), lambda i, j: (i, 0)),
        ],
        scratch_shapes=[pltpu.VMEM((tm, d), BF16)],
        compiler_params=_params(("arbitrary", "arbitrary")),
        name="inproj",
    )(x, gain.reshape(1, d), w, w)


def _normmm_kernel(x_ref, g_ref, wk_ref, wv_ref, k_ref, v_ref, xn_ref):
    @pl.when(pl.program_id(0) == 0)
    def _():
        xn_ref[...] = _rms(x_ref[...], g_ref[...]).astype(BF16)

    xn = xn_ref[...]
    k_ref[...] = _dot(xn, wk_ref[...]).astype(BF16)
    v_ref[...] = _dot(xn, wv_ref[...]).astype(BF16)


def _normmm(x, gain, wk, wv, *, tn=1024):
    t, d = x.shape
    n = wk.shape[1]
    tn = min(tn, n)
    col = lambda rows: pl.BlockSpec((rows, tn), lambda j: (0, j))
    return pl.pallas_call(
        _normmm_kernel,
        out_shape=[jax.ShapeDtypeStruct((t, n), BF16)] * 2,
        grid=(n // tn,),
        in_specs=[_resident((t, d)), _resident((1, d)), col(d), col(d)],
        out_specs=[col(t), col(t)],
        scratch_shapes=[pltpu.VMEM((t, d), BF16)],
        compiler_params=_params(("arbitrary",)),
        name="normmm",
    )(x, gain.reshape(1, d), wk, wv)


def _s5_kernel(u_ref, m_ref, win_ref, wout_ref, lam_ref, d_ref, y_ref, z_ref, s_ref, *, tc, nb):
    rb = z_ref.shape[0] // nb
    seq = rb * tc
    u32 = jnp.concatenate(
        [jnp.concatenate([u_ref[pl.ds(b * seq + j, rb, stride=tc), :] for j in range(tc)], axis=-1)
         for b in range(nb)], axis=0)
    u = u32.astype(BF16)
    z_ref[...] = _dot(u, win_ref[0])
    ns = z_ref.shape[1] // 2
    lam_re = lam_ref[0, :, :ns]
    lam_im = lam_ref[0, :, ns:]

    def step(c, carry):
        out = []
        for b in range(nb):
            s_re, s_im = carry[2 * b], carry[2 * b + 1]
            row = pl.ds(b * rb + c, 1)
            s_ref[row, :ns] = s_re
            s_ref[row, ns:] = s_im
            out += [lam_re * s_re - lam_im * s_im + z_ref[row, :ns],
                    lam_re * s_im + lam_im * s_re + z_ref[row, ns:]]
        return tuple(out)

    zero = jnp.zeros((1, ns), F32)
    lax.fori_loop(0, rb, step, (zero,) * (2 * nb))

    s = s_ref[...].astype(BF16)
    per_tile = MXU_TILE // LANES
    for cb in range(tc // per_tile):
        cols = slice(cb * MXU_TILE, (cb + 1) * MXU_TILE)
        kmax = (cb + 1) * MXU_TILE
        y = (_dot(u[:, :kmax], m_ref[0, :kmax, cols]) + _dot(s, wout_ref[0, :, cols])
             + d_ref[0, :, cols] * u32[:, cols])
        y = jax.nn.gelu(y)
        for b in range(nb):
            for i in range(per_tile):
                y_ref[pl.ds(b * seq + cb * per_tile + i, rb, stride=tc), :] = (
                    y[b * rb:(b + 1) * rb, i * LANES:(i + 1) * LANES])


def _s5(u, m, w_in, w_out, lam, dtile, batch, *, nb=1):
    t, sw = u.shape
    seq = t // batch
    tc = S5_CHUNK
    nlb, width, nstate = w_in.shape
    nb = nb if batch % nb == 0 else 1
    rows = nb * seq // tc
    return pl.pallas_call(
        functools.partial(_s5_kernel, tc=tc, nb=nb),
        out_shape=jax.ShapeDtypeStruct((t, sw), F32),
        grid=(nlb, batch // nb),
        in_specs=[
            pl.BlockSpec((nb * seq, LANES), lambda k, b: (b, k)),
            pl.BlockSpec((1, width, width), lambda k, b: (k, 0, 0)),
            pl.BlockSpec((1, width, nstate), lambda k, b: (k, 0, 0)),
            pl.BlockSpec((1, nstate, width), lambda k, b: (k, 0, 0)),
            pl.BlockSpec((1, 1, nstate), lambda k, b: (k, 0, 0)),
            pl.BlockSpec((1, 1, width), lambda k, b: (k, 0, 0)),
        ],
        out_specs=pl.BlockSpec((nb * seq, LANES), lambda k, b: (b, k)),
        scratch_shapes=[pltpu.VMEM((rows, nstate), F32), pltpu.VMEM((rows, nstate), F32)],
        compiler_params=_params(("parallel", "arbitrary")),
        name="s5",
    )(u, m, w_in, w_out, lam, dtile)


def _log2(n):
    assert n & (n - 1) == 0, n
    return n.bit_length() - 1


def _s5prep_kernel(bre_ref, bim_ref, cre_ref, cim_ref, lrow_ref, lcol_ref,
                   toep_ref, win_ref, wout_ref, lam_ref, *, tc):
    hi = lax.Precision.HIGHEST
    ns, gi = bre_ref.shape[1], bre_ref.shape[2]
    lanes, p = cre_ref.shape[1], cre_ref.shape[2]

    def iota(shape, dim):
        return lax.broadcasted_iota(jnp.int32, shape, dim)

    def grp(x, size):
        return lax.shift_right_logical(x, _log2(size))

    e_i = ((iota((gi, lanes), 1) & (gi - 1)) == iota((gi, lanes), 0)).astype(F32)
    e_p = ((iota((p, ns), 1) & (p - 1)) == iota((p, ns), 0)).astype(F32)
    mask_b = grp(iota((ns, lanes), 0), p) == grp(iota((ns, lanes), 1), gi)
    mask_c = grp(iota((lanes, ns), 0), gi) == grp(iota((lanes, ns), 1), p)

    def bdiag_b(ref):
        return jnp.where(mask_b, jnp.dot(ref[0], e_i, precision=hi,
                                         preferred_element_type=F32), 0.0).T

    def bdiag_c(ref):
        return jnp.where(mask_c, jnp.dot(ref[0], e_p, precision=hi,
                                         preferred_element_type=F32), 0.0).T

    b_re, b_im = bdiag_b(bre_ref), bdiag_b(bim_ref)
    c_re, c_im = bdiag_c(cre_ref), bdiag_c(cim_ref)
    l_re, l_im = lrow_ref[0, 0:1, :], lrow_ref[0, 1:2, :]
    lc_re, lc_im = lcol_ref[0, :, 0:1], lcol_ref[0, :, 1:2]
    p_re, p_im = jnp.ones_like(l_re), jnp.zeros_like(l_im)
    q_re, q_im = jnp.ones_like(lc_re), jnp.zeros_like(lc_im)
    zero_tile = jnp.zeros((lanes, lanes), BF16)

    def tile(i):
        return slice(i * lanes, (i + 1) * lanes)

    for n in range(tc):
        lb_re = b_re * p_re - b_im * p_im
        lb_im = b_re * p_im + b_im * p_re
        win_ref[0, tile(tc - 1 - n), :ns] = lb_re.astype(BF16)
        win_ref[0, tile(tc - 1 - n), ns:] = lb_im.astype(BF16)
        kern = (jnp.dot(lb_re, c_re, precision=hi, preferred_element_type=F32)
                - jnp.dot(lb_im, c_im, precision=hi, preferred_element_type=F32)).astype(BF16)
        for j in range(tc - n):
            toep_ref[0, tile(j), tile(j + n)] = kern
        if n:
            for t in range(tc - n):
                toep_ref[0, tile(t + n), tile(t)] = zero_tile
        p_re, p_im = p_re * l_re - p_im * l_im, p_re * l_im + p_im * l_re
        q_re, q_im = q_re * lc_re - q_im * lc_im, q_re * lc_im + q_im * lc_re
        wout_ref[0, :ns, tile(n)] = (c_re * q_re - c_im * q_im).astype(BF16)
        wout_ref[0, ns:, tile(n)] = (-(c_re * q_im + c_im * q_re)).astype(BF16)
    lam_ref[0, :, :ns] = p_re
    lam_ref[0, :, ns:] = p_im


def _s5_weights(a_re, a_im, log_dt, b_re, b_im, c_re, c_im, d_skip):
    tc = S5_CHUNK
    g, p = a_re.shape
    gl = GROUPS_PER_LANE_BLOCK
    nlb = g // gl
    ns = gl * p
    dt = jnp.exp(log_dt)[:, None]
    mag = jnp.exp(a_re * dt)
    l_re = mag * jnp.cos(a_im * dt)
    l_im = mag * jnp.sin(a_im * dt)
    den = a_re * a_re + a_im * a_im
    n_re = l_re - 1.0
    n_im = l_im
    f_re = (n_re * a_re + n_im * a_im) / den
    f_im = (n_im * a_re - n_re * a_im) / den
    bb_re = (f_re[..., None] * b_re - f_im[..., None] * b_im).reshape(nlb, ns, S5_GROUP)
    bb_im = (f_re[..., None] * b_im + f_im[..., None] * b_re).reshape(nlb, ns, S5_GROUP)
    lrow = jnp.stack([l_re.reshape(nlb, ns), l_im.reshape(nlb, ns)], axis=1)
    lcol = jnp.stack([l_re.reshape(nlb, ns), l_im.reshape(nlb, ns)], axis=2)
    width = tc * LANES
    blk = lambda *shape: pl.BlockSpec((1,) + shape, lambda k: (k, 0, 0))
    toep, w_in, w_out, lam = pl.pallas_call(
        functools.partial(_s5prep_kernel, tc=tc),
        out_shape=(jax.ShapeDtypeStruct((nlb, width, width), BF16),
                   jax.ShapeDtypeStruct((nlb, width, 2 * ns), BF16),
                   jax.ShapeDtypeStruct((nlb, 2 * ns, width), BF16),
                   jax.ShapeDtypeStruct((nlb, 1, 2 * ns), F32)),
        grid=(nlb,),
        in_specs=[blk(ns, S5_GROUP), blk(ns, S5_GROUP), blk(LANES, p), blk(LANES, p),
                  blk(2, ns), blk(ns, 2)],
        out_specs=(blk(width, width), blk(width, 2 * ns), blk(2 * ns, width),
                   blk(1, 2 * ns)),
        compiler_params=_params(("parallel",)),
        name="s5prep",
    )(bb_re, bb_im, c_re.reshape(nlb, LANES, p), c_im.reshape(nlb, LANES, p), lrow, lcol)
    dtile = jnp.tile(d_skip.reshape(nlb, 1, LANES), (1, 1, tc))
    return toep, w_in, w_out, lam, dtile


def _ret_kernel(q_ref, k_ref, v_ref, g_ref, cos_ref, sin_ref, dec_ref, xi_ref,
                zeta_ref, gam_ref, o_ref, state_ref, *, heads):
    c = pl.program_id(1)

    @pl.when(c == 0)
    def _():
        state_ref[...] = jnp.zeros_like(state_ref)

    dqk = q_ref.shape[1] // heads
    dv = v_ref.shape[1] // heads
    half = dqk // 2
    cos = cos_ref[...]
    sin = sin_ref[...]

    def rot(t):
        t1, t2 = t[:, :half], t[:, half:]
        return jnp.concatenate([t1 * cos - t2 * sin, t1 * sin + t2 * cos], axis=-1)

    for h in range(heads):
        q = rot(q_ref[:, h * dqk:(h + 1) * dqk].astype(F32))
        k = rot(k_ref[:, h * dqk:(h + 1) * dqk].astype(F32)) * (dqk ** -0.5)
        v = v_ref[:, h * dv:(h + 1) * dv]
        qb = q.astype(BF16)
        scores = _dot_nt(qb, k.astype(BF16)) * dec_ref[h]
        st = state_ref[h]
        out = _dot(scores.astype(BF16), v) + _dot(qb, st.astype(BF16)) * xi_ref[h]
        kz = (k * zeta_ref[h]).astype(BF16)
        state_ref[h] = st * gam_ref[h] + _dot_tn(kz, v)
        mean = jnp.mean(out, axis=-1, keepdims=True)
        cen = out - mean
        var = jnp.mean(cen * cen, axis=-1, keepdims=True)
        normed = cen * lax.rsqrt(var + GN_EPS)
        g = g_ref[:, h * dv:(h + 1) * dv].astype(F32)
        o_ref[:, h * dv:(h + 1) * dv] = (g * jax.nn.sigmoid(g) * normed).astype(BF16)


def _retention(proj, d, batch, seq):
    t = proj.shape[0]
    heads = RET_HEADS
    ch = min(RET_KERNEL_CHUNK, seq)
    nc = seq // ch
    dqk = d // 2 // heads
    dv = d // heads
    half = dqk // 2
    inv = ROPE_BASE ** (-jnp.arange(0, dqk, 2, dtype=F32) / dqk)
    ang = jnp.arange(seq, dtype=F32)[:, None] * inv[None, :]
    cos, sin = jnp.cos(ang), jnp.sin(ang)
    log_gamma = jnp.log(1.0 - 2.0 ** (-5.0 - jnp.arange(heads, dtype=F32)))
    idx = jnp.arange(ch, dtype=F32)
    rel = idx[:, None] - idx[None, :]
    dec = jnp.where(rel[None] >= 0,
                    jnp.exp(jnp.maximum(rel, 0.0)[None] * log_gamma[:, None, None]), 0.0)
    xi = jnp.exp((idx + 1.0)[None, :] * log_gamma[:, None])[:, :, None]
    zeta = jnp.exp((ch - 1.0 - idx)[None, :] * log_gamma[:, None])[:, :, None]
    gam = jnp.exp(ch * log_gamma)[:, None, None]
    row = lambda b, c: b * nc + c
    return pl.pallas_call(
        functools.partial(_ret_kernel, heads=heads),
        out_shape=jax.ShapeDtypeStruct((t, d), BF16),
        grid=(batch, nc),
        in_specs=[
            pl.BlockSpec((ch, d // 2), lambda b, c: (row(b, c), 0)),
            pl.BlockSpec((ch, d // 2), lambda b, c: (row(b, c), 1)),
            pl.BlockSpec((ch, d), lambda b, c: (row(b, c), 1)),
            pl.BlockSpec((ch, d), lambda b, c: (row(b, c), 2)),
            pl.BlockSpec((ch, half), lambda b, c: (c, 0)),
            pl.BlockSpec((ch, half), lambda b, c: (c, 0)),
            pl.BlockSpec((heads, ch, ch), lambda b, c: (0, 0, 0)),
            pl.BlockSpec((heads, ch, 1), lambda b, c: (0, 0, 0)),
            pl.BlockSpec((heads, ch, 1), lambda b, c: (0, 0, 0)),
            pl.BlockSpec((heads, 1, 1), lambda b, c: (0, 0, 0)),
        ],
        out_specs=pl.BlockSpec((ch, d), lambda b, c: (row(b, c), 0)),
        scratch_shapes=[pltpu.VMEM((heads, dqk, dv), F32)],
        compiler_params=_params(("parallel", "arbitrary")),
        name="retention",
    )(proj, proj, proj, proj, cos, sin, dec, xi, zeta, gam)


def _mix_kernel(*refs, cn, ncast):
    og_ref, yg_ref, ga_ref, gb_ref, h_ref, wo_ref, wv_ref, wg_ref, wout_ref = refs[:9]
    cast_in = refs[9:9 + ncast]
    o_ref = refs[9 + ncast]
    cast_out = refs[10 + ncast:]
    _run_casts(cast_in, cast_out)
    d = h_ref.shape[1]
    og = og_ref[...]
    yg = yg_ref[...].astype(BF16)
    acc = h_ref[...]
    for j in range(d // cn):
        cols = slice(j * cn, (j + 1) * cn)
        y_b = _dot(og, wo_ref[:, cols])
        y_a = _dot(yg, wv_ref[:, cols]) * jax.nn.sigmoid(_dot(yg, wg_ref[:, cols]))
        merged = (jax.nn.sigmoid(ga_ref[:, cols].astype(F32)) * y_a
                  + jax.nn.sigmoid(gb_ref[:, cols].astype(F32)) * y_b)
        acc = acc + _dot(merged.astype(BF16), wout_ref[cols, :])
    o_ref[...] = acc


def _mix(og, yg, proj, h, w_o, w_v, w_g, w_out, casts=(), *, tm=256, cn=512):
    t, d = h.shape
    sw = yg.shape[1]
    tm, cn = min(tm, t), min(cn, d)
    cast_specs = [pl.BlockSpec(*_cast_block(c.shape, t // tm, 1)) for c in casts]
    return pl.pallas_call(
        functools.partial(_mix_kernel, cn=cn, ncast=len(casts)),
        out_shape=[jax.ShapeDtypeStruct((t, d), F32)]
        + [jax.ShapeDtypeStruct(c.shape, BF16) for c in casts],
        grid=(t // tm, 1),
        in_specs=[
            pl.BlockSpec((tm, d), lambda i, j: (i, 0)),
            pl.BlockSpec((tm, sw), lambda i, j: (i, 0)),
            pl.BlockSpec((tm, d), lambda i, j: (i, 3)),
            pl.BlockSpec((tm, d), lambda i, j: (i, 4)),
            pl.BlockSpec((tm, d), lambda i, j: (i, 0)),
            _resident((d, d)), _resident((sw, d)), _resident((sw, d)), _resident((d, d)),
        ] + cast_specs,
        out_specs=[pl.BlockSpec((tm, d), lambda i, j: (i, 0))] + cast_specs,
        compiler_params=_params(("arbitrary", "arbitrary")),
        name="mix",
    )(og, yg, proj, proj, h, w_o, w_v, w_g, w_out, *casts)


def _xattn_kernel(h_ref, g_ref, wq_ref, k_ref, v_ref, wo_ref, o_ref, *, heads):
    h = h_ref[...]
    d = h.shape[1]
    dh = d // heads
    xn = _rms(h, g_ref[...]).astype(BF16)
    outs = []
    for hd in range(heads):
        cols = slice(hd * dh, (hd + 1) * dh)
        q = _dot(xn, wq_ref[:, cols])
        s = _dot_nt(q.astype(BF16), k_ref[0, :, cols]) * (dh ** -0.5)
        e = jnp.exp(s - jnp.max(s, axis=-1, keepdims=True))
        p = e / jnp.sum(e, axis=-1, keepdims=True)
        outs.append(_dot(p.astype(BF16), v_ref[0, :, cols]).astype(BF16))
    o_ref[...] = h + _dot(jnp.concatenate(outs, axis=-1), wo_ref[...])


def _xattn(h, gain, wq, kmem, vmem, wo, batch, *, tm=1024):
    t, d = h.shape
    seq = t // batch
    mlen = kmem.shape[1]
    tm = min(tm, seq)
    nt = seq // tm
    return pl.pallas_call(
        functools.partial(_xattn_kernel, heads=XATTN_HEADS),
        out_shape=jax.ShapeDtypeStruct((t, d), F32),
        grid=(batch, nt),
        in_specs=[
            pl.BlockSpec((tm, d), lambda b, i: (b * nt + i, 0)),
            _resident((1, d)),
            _resident((d, d)),
            pl.BlockSpec((1, mlen, d), lambda b, i: (b, 0, 0)),
            pl.BlockSpec((1, mlen, d), lambda b, i: (b, 0, 0)),
            _resident((d, d)),
        ],
        out_specs=pl.BlockSpec((tm, d), lambda b, i: (b * nt + i, 0)),
        compiler_params=_params(("arbitrary", "arbitrary")),
        name="xattn",
    )(h, gain.reshape(1, d), wq, kmem, vmem, wo)


def kernel(x, mem, ffn1_norm, ffn1_w1, ffn1_w3, ffn1_w2, mix_norm, w_in, s5_a_re, s5_a_im, s5_log_dt, s5_b_re, s5_b_im, s5_c_re, s5_c_im, s5_d, s5_glu_v, s5_glu_g, ret_w_o, w_out, xattn_norm, mem_norm, xattn_wq, xattn_wk, xattn_wv, xattn_wo, ffn2_norm, ffn2_w1, ffn2_w3, ffn2_w2, final_norm):
    batch, seq, d = x.shape
    mlen = mem.shape[1]
    depth = ffn1_w1.shape[0]
    t = batch * seq
    s5_width = s5_d.shape[1]
    bf = lambda w: w.astype(BF16)

    h = x.reshape(t, d)
    mem2 = mem.reshape(batch * mlen, d)
    for l in range(depth):
        last = l == depth - 1
        later = (w_in, s5_glu_v, s5_glu_g, ret_w_o, w_out, xattn_wq, xattn_wk, xattn_wv, xattn_wo)
        h, c_in, c_gv, c_gg, c_ro, c_out, c_wq, c_wk, c_wv, c_wo = _ffn(
            h, ffn1_norm[l], bf(ffn1_w1[l]), bf(ffn1_w3[l]), bf(ffn1_w2[l]),
            casts=[w[l] for w in later])
        proj, us5 = _inproj(h, mix_norm[l], c_in, s5_width)
        toep, s_in, s_out, lam, dtile = _s5_weights(
            s5_a_re[l], s5_a_im[l], s5_log_dt[l], s5_b_re[l], s5_b_im[l],
            s5_c_re[l], s5_c_im[l], s5_d[l])
        yg = _s5(us5, toep, s_in, s_out, lam, dtile, batch)
        og = _retention(proj, d, batch, seq)
        h, c_f1, c_f3, c_f2 = _mix(og, yg, proj, h, c_ro, c_gv, c_gg, c_out,
                                   casts=[ffn2_w1[l], ffn2_w3[l], ffn2_w2[l]])
        kmem, vmem = _normmm(mem2, mem_norm[l], c_wk, c_wv)
        h = _xattn(h, xattn_norm[l], c_wq, kmem.reshape(batch, mlen, d),
                   vmem.reshape(batch, mlen, d), c_wo, batch)
        h, = _ffn(h, ffn2_norm[l], c_f1, c_f3, c_f2, final_norm if last else None)
    if depth == 0:
        raise ValueError("depth must be >= 1")
    return h.reshape(batch, seq, d)
```

```python
import functools

import jax
import jax.numpy as jnp
from jax import lax
from jax.experimental import pallas as pl
from jax.experimental.pallas import tpu as pltpu

F32 = jnp.float32
BF16 = jnp.bfloat16

RMS_EPS = 1e-6
GN_EPS = 1e-5
ROPE_BASE = 10000.0
S5_GROUP = 16
RET_HEADS = 4
XATTN_HEADS = 4
RET_KERNEL_CHUNK = 256
S5_CHUNK = 16
LANES = 128
GROUPS_PER_LANE_BLOCK = LANES // S5_GROUP
VMEM_LIMIT_BYTES = 60 * 1024 * 1024
MXU_TILE = 256


def _params(semantics):
    return pltpu.CompilerParams(dimension_semantics=semantics,
                                vmem_limit_bytes=VMEM_LIMIT_BYTES)


def _resident(shape):
    return pl.BlockSpec(shape, lambda *_: (0,) * len(shape), pipeline_mode=pl.Buffered(1))


def _rms(x, gain):
    ms = jnp.mean(x * x, axis=-1, keepdims=True)
    return x * lax.rsqrt(ms + RMS_EPS) * gain


def _dot(a, b):
    return jnp.dot(a, b, preferred_element_type=F32)


def _dot_nt(a, b):
    return lax.dot_general(a, b, (((1,), (1,)), ((), ())), preferred_element_type=F32)


def _dot_tn(a, b):
    return lax.dot_general(a, b, (((0,), (0,)), ((), ())), preferred_element_type=F32)


def _run_casts(cast_in, cast_out):
    for src_ref, dst_ref in zip(cast_in, cast_out):
        dst_ref[...] = src_ref[...].astype(BF16)


def _ffn_kernel(*refs, final, ncast):
    nin = 6 if final else 5
    x_ref, g_ref, w1_ref, w3_ref, w2_ref = refs[:5]
    fg_ref = refs[5] if final else None
    cast_in = refs[nin:nin + ncast]
    o_ref = refs[nin + ncast]
    cast_out = refs[nin + ncast + 1:nin + 2 * ncast + 1]
    xn_ref = refs[-1]
    j = pl.program_id(1)

    _run_casts(cast_in, cast_out)

    @pl.when(j == 0)
    def _():
        x = x_ref[...]
        xn_ref[...] = _rms(x, g_ref[...]).astype(BF16)
        o_ref[...] = x

    xn = xn_ref[...]
    a = _dot(xn, w1_ref[...])
    b = _dot(xn, w3_ref[...])
    mid = (a * jax.nn.sigmoid(a) * b * 0.5).astype(BF16)
    o_ref[...] += _dot(mid, w2_ref[...])

    if final:
        @pl.when(j == pl.num_programs(1) - 1)
        def _():
            o_ref[...] = _rms(o_ref[...], fg_ref[...])


def _first_step_only(nt):
    return lambda i, j: (jnp.minimum(i + jnp.minimum(j, 1), nt - 1), 0)


def _cast_block(shape, nt, nf):
    r, c = shape

    def tiled(rows, cols):
        return rows % 16 == 0 and cols % LANES == 0

    if r % nt == 0 and c % nf == 0 and tiled(r // nt, c // nf):
        return (r // nt, c // nf), lambda i, j: (i, j)
    if r % nf == 0 and c % nt == 0 and tiled(r // nf, c // nt):
        return (r // nf, c // nt), lambda i, j: (j, i)
    nrow = max(n for n in range(1, nt + 1) if r % n == 0 and (r // n) % 16 == 0)
    ncol = max(n for n in range(1, nf + 1) if c % n == 0 and (c // n) % LANES == 0)
    return ((r // nrow, c // ncol),
            lambda i, j: (jnp.minimum(i, nrow - 1), jnp.minimum(j, ncol - 1)))


def _ffn(x, gain, w1, w3, w2, final_gain=None, casts=(), *, tm=1024, tf=512):
    t, d = x.shape
    f = w1.shape[1]
    tm, tf = min(tm, t), min(tf, f)
    final = final_gain is not None
    in_specs = [
        pl.BlockSpec((tm, d), _first_step_only(t // tm)),
        pl.BlockSpec((1, d), lambda i, j: (0, 0)),
        pl.BlockSpec((d, tf), lambda i, j: (0, j)),
        pl.BlockSpec((d, tf), lambda i, j: (0, j)),
        pl.BlockSpec((tf, d), lambda i, j: (j, 0)),
    ]
    args = [x, gain.reshape(1, d), w1, w3, w2]
    if final:
        in_specs.append(pl.BlockSpec((1, d), lambda i, j: (0, 0)))
        args.append(final_gain.reshape(1, d))
    cast_specs = [pl.BlockSpec(*_cast_block(w.shape, t // tm, f // tf)) for w in casts]
    outs = pl.pallas_call(
        functools.partial(_ffn_kernel, final=final, ncast=len(casts)),
        out_shape=[jax.ShapeDtypeStruct((t, d), F32)]
        + [jax.ShapeDtypeStruct(w.shape, BF16) for w in casts],
        grid=(t // tm, f // tf),
        in_specs=in_specs + cast_specs,
        out_specs=[pl.BlockSpec((tm, d), lambda i, j: (i, 0))] + cast_specs,
        scratch_shapes=[pltpu.VMEM((tm, d), BF16)],
        compiler_params=_params(("arbitrary", "arbitrary")),
        name="ffn_final" if final else "ffn",
    )(*args, *casts)
    return outs


def _inproj_kernel(x_ref, g_ref, wa_ref, wb_ref, proj_ref, us5_ref, xn_ref):
    j = pl.program_id(1)
    tn = wa_ref.shape[1]

    @pl.when(j == 0)
    def _():
        xn = _rms(x_ref[...], g_ref[...]).astype(BF16)
        xn_ref[...] = xn
        res = _dot(xn, wa_ref[...])
        for k in range(us5_ref.shape[0]):
            us5_ref[k] = res[:, k * LANES:(k + 1) * LANES]

    @pl.when(j > 0)
    def _():
        xn = xn_ref[...]
        proj_ref[:, :tn] = _dot(xn, wa_ref[...]).astype(BF16)
        proj_ref[:, tn:] = _dot(xn, wb_ref[...]).astype(BF16)


def _inproj(x, gain, w, s5_width, *, tm=1024):
    t, d = x.shape
    n = w.shape[1]
    tn = s5_width
    tm = min(tm, t)
    pairs, odd = divmod(n // tn - 1, 2)
    assert odd == 0, (n, tn)
    nt, nf = t // tm, 1 + pairs
    return pl.pallas_call(
        _inproj_kernel,
        out_shape=[jax.ShapeDtypeStruct((t, n - tn), BF16),
                   jax.ShapeDtypeStruct((tn // LANES, t, LANES), F32)],
        grid=(nt, nf),
        in_specs=[
            pl.BlockSpec((tm, d), _first_step_only(nt)),
            pl.BlockSpec((1, d), lambda i, j: (0, 0)),
            pl.BlockSpec((d, tn), lambda i, j: (0, jnp.maximum(2 * j - 1, 0))),
            pl.BlockSpec((d, tn), lambda i, j: (0, 2 * jnp.maximum(j, 1))),
        ],
        out_specs=[
            pl.BlockSpec((tm, 2 * tn), lambda i, j: (i, jnp.maximum(j - 1, 0))),
            pl.BlockSpec((tn // LANES, tm, LANES), lambda i, j: (0, i, 0)),
        ],
        scratch_shapes=[pltpu.VMEM((tm, d), BF16)],
        compiler_params=_params(("arbitrary", "arbitrary")),
        name="inproj",
    )(x, gain.reshape(1, d), w, w)


def _normmm_kernel(x_ref, g_ref, wk_ref, wv_ref, k_ref, v_ref, xn_ref):
    @pl.when(pl.program_id(0) == 0)
    def _():
        xn_ref[...] = _rms(x_ref[...], g_ref[...]).astype(BF16)

    xn = xn_ref[...]
    k_ref[...] = _dot(xn, wk_ref[...]).astype(BF16)
    v_ref[...] = _dot(xn, wv_ref[...]).astype(BF16)


def _normmm(x, gain, wk, wv, *, tn=1024):
    t, d = x.shape
    n = wk.shape[1]
    tn = min(tn, n)
    col = lambda rows: pl.BlockSpec((rows, tn), lambda j: (0, j))
    return pl.pallas_call(
        _normmm_kernel,
        out_shape=[jax.ShapeDtypeStruct((t, n), BF16)] * 2,
        grid=(n // tn,),
        in_specs=[_resident((t, d)), _resident((1, d)), col(d), col(d)],
        out_specs=[col(t), col(t)],
        scratch_shapes=[pltpu.VMEM((t, d), BF16)],
        compiler_params=_params(("arbitrary",)),
        name="normmm",
    )(x, gain.reshape(1, d), wk, wv)


def _s5_kernel(u_ref, m_ref, win_ref, wout_ref, lam_ref, d_ref, y_ref, z_ref, s_ref, *, tc, nb):
    rb = z_ref.shape[0] // nb
    seq = rb * tc
    u32 = jnp.concatenate(
        [jnp.concatenate([u_ref[pl.ds(b * seq + j, rb, stride=tc), :] for j in range(tc)], axis=-1)
         for b in range(nb)], axis=0)
    u = u32.astype(BF16)
    z_ref[...] = _dot(u, win_ref[0])
    ns = z_ref.shape[1] // 2
    lam_re = lam_ref[0, :, :ns]
    lam_im = lam_ref[0, :, ns:]

    def step(c, carry):
        out = []
        for b in range(nb):
            s_re, s_im = carry[2 * b], carry[2 * b + 1]
            row = pl.ds(b * rb + c, 1)
            s_ref[row, :ns] = s_re
            s_ref[row, ns:] = s_im
            out += [lam_re * s_re - lam_im * s_im + z_ref[row, :ns],
                    lam_re * s_im + lam_im * s_re + z_ref[row, ns:]]
        return tuple(out)

    zero = jnp.zeros((1, ns), F32)
    lax.fori_loop(0, rb, step, (zero,) * (2 * nb))

    s = s_ref[...].astype(BF16)
    per_tile = MXU_TILE // LANES
    for cb in range(tc // per_tile):
        cols = slice(cb * MXU_TILE, (cb + 1) * MXU_TILE)
        kmax = (cb + 1) * MXU_TILE
        y = (_dot(u[:, :kmax], m_ref[0, :kmax, cols]) + _dot(s, wout_ref[0, :, cols])
             + d_ref[0, :, cols] * u32[:, cols])
        y = jax.nn.gelu(y)
        for b in range(nb):
            for i in range(per_tile):
                y_ref[pl.ds(b * seq + cb * per_tile + i, rb, stride=tc), :] = (
                    y[b * rb:(b + 1) * rb, i * LANES:(i + 1) * LANES])


def _s5(u, m, w_in, w_out, lam, dtile, batch, *, nb=1):
    nlb, t, _ = u.shape
    seq = t // batch
    tc = S5_CHUNK
    _, width, nstate = w_in.shape
    nb = nb if batch % nb == 0 else 1
    rows = nb * seq // tc
    return pl.pallas_call(
        functools.partial(_s5_kernel, tc=tc, nb=nb),
        out_shape=jax.ShapeDtypeStruct((nlb, t, LANES), F32),
        grid=(nlb, batch // nb),
        in_specs=[
            pl.BlockSpec((None, nb * seq, LANES), lambda k, b: (k, b, 0)),
            pl.BlockSpec((1, width, width), lambda k, b: (k, 0, 0)),
            pl.BlockSpec((1, width, nstate), lambda k, b: (k, 0, 0)),
            pl.BlockSpec((1, nstate, width), lambda k, b: (k, 0, 0)),
            pl.BlockSpec((1, 1, nstate), lambda k, b: (k, 0, 0)),
            pl.BlockSpec((1, 1, width), lambda k, b: (k, 0, 0)),
        ],
        out_specs=pl.BlockSpec((None, nb * seq, LANES), lambda k, b: (k, b, 0)),
        scratch_shapes=[pltpu.VMEM((rows, nstate), F32), pltpu.VMEM((rows, nstate), F32)],
        compiler_params=_params(("parallel", "arbitrary")),
        name="s5",
    )(u, m, w_in, w_out, lam, dtile)


def _log2(n):
    assert n & (n - 1) == 0, n
    return n.bit_length() - 1


def _s5prep_kernel(bre_ref, bim_ref, cre_ref, cim_ref, lrow_ref, lcol_ref,
                   toep_ref, win_ref, wout_ref, lam_ref, *, tc):
    ns, gi = bre_ref.shape[1], bre_ref.shape[2]
    lanes, p = cre_ref.shape[1], cre_ref.shape[2]

    def iota(shape, dim):
        return lax.broadcasted_iota(jnp.int32, shape, dim)

    def grp(x, size):
        return lax.shift_right_logical(x, _log2(size))

    def split(a):
        hi = a.astype(BF16)
        return hi, (a - hi.astype(F32)).astype(BF16)

    def dot3(a, b):
        return _dot(a[0], b[0]) + _dot(a[0], b[1]) + _dot(a[1], b[0])

    e_i = ((iota((gi, lanes), 1) & (gi - 1)) == iota((gi, lanes), 0)).astype(BF16)
    e_p = ((iota((p, ns), 1) & (p - 1)) == iota((p, ns), 0)).astype(BF16)

    def repeat(a, e):
        hi, lo = split(a)
        return _dot(hi, e) + _dot(lo, e)

    mask_b = grp(iota((ns, lanes), 0), p) == grp(iota((ns, lanes), 1), gi)
    mask_c = grp(iota((lanes, ns), 0), gi) == grp(iota((lanes, ns), 1), p)

    def bdiag_b(ref):
        return jnp.where(mask_b, repeat(ref[0], e_i), 0.0).T

    def bdiag_c(ref):
        return jnp.where(mask_c, repeat(ref[0], e_p), 0.0).T

    b_re, b_im = bdiag_b(bre_ref), bdiag_b(bim_ref)
    c_re, c_im = bdiag_c(cre_ref), bdiag_c(cim_ref)
    c_re2, c_im2 = split(c_re), split(c_im)
    l_re, l_im = lrow_ref[0, 0:1, :], lrow_ref[0, 1:2, :]
    lc_re, lc_im = lcol_ref[0, :, 0:1], lcol_ref[0, :, 1:2]
    p_re, p_im = jnp.ones_like(l_re), jnp.zeros_like(l_im)
    q_re, q_im = jnp.ones_like(lc_re), jnp.zeros_like(lc_im)
    zero_tile = jnp.zeros((lanes, lanes), BF16)

    def tile(i):
        return slice(i * lanes, (i + 1) * lanes)

    for n in range(tc):
        lb_re = b_re * p_re - b_im * p_im
        lb_im = b_re * p_im + b_im * p_re
        win_ref[0, tile(tc - 1 - n), :ns] = lb_re.astype(BF16)
        win_ref[0, tile(tc - 1 - n), ns:] = lb_im.astype(BF16)
        kern = (dot3(split(lb_re), c_re2) - dot3(split(lb_im), c_im2)).astype(BF16)
        for j in range(tc - n):
            toep_ref[0, tile(j), tile(j + n)] = kern
        if n:
            for t in range(tc - n):
                toep_ref[0, tile(t + n), tile(t)] = zero_tile
        p_re, p_im = p_re * l_re - p_im * l_im, p_re * l_im + p_im * l_re
        q_re, q_im = q_re * lc_re - q_im * lc_im, q_re * lc_im + q_im * lc_re
        wout_ref[0, :ns, tile(n)] = (c_re * q_re - c_im * q_im).astype(BF16)
        wout_ref[0, ns:, tile(n)] = (-(c_re * q_im + c_im * q_re)).astype(BF16)
    lam_ref[0, :, :ns] = p_re
    lam_ref[0, :, ns:] = p_im


def _s5_weights(a_re, a_im, log_dt, b_re, b_im, c_re, c_im, d_skip):
    tc = S5_CHUNK
    g, p = a_re.shape
    gl = GROUPS_PER_LANE_BLOCK
    nlb = g // gl
    ns = gl * p
    dt = jnp.exp(log_dt)[:, None]
    mag = jnp.exp(a_re * dt)
    l_re = mag * jnp.cos(a_im * dt)
    l_im = mag * jnp.sin(a_im * dt)
    den = a_re * a_re + a_im * a_im
    n_re = l_re - 1.0
    n_im = l_im
    f_re = (n_re * a_re + n_im * a_im) / den
    f_im = (n_im * a_re - n_re * a_im) / den
    bb_re = (f_re[..., None] * b_re - f_im[..., None] * b_im).reshape(nlb, ns, S5_GROUP)
    bb_im = (f_re[..., None] * b_im + f_im[..., None] * b_re).reshape(nlb, ns, S5_GROUP)
    lrow = jnp.stack([l_re.reshape(nlb, ns), l_im.reshape(nlb, ns)], axis=1)
    lcol = jnp.stack([l_re.reshape(nlb, ns), l_im.reshape(nlb, ns)], axis=2)
    width = tc * LANES
    blk = lambda *shape: pl.BlockSpec((1,) + shape, lambda k: (k, 0, 0))
    toep, w_in, w_out, lam = pl.pallas_call(
        functools.partial(_s5prep_kernel, tc=tc),
        out_shape=(jax.ShapeDtypeStruct((nlb, width, width), BF16),
                   jax.ShapeDtypeStruct((nlb, width, 2 * ns), BF16),
                   jax.ShapeDtypeStruct((nlb, 2 * ns, width), BF16),
                   jax.ShapeDtypeStruct((nlb, 1, 2 * ns), F32)),
        grid=(nlb,),
        in_specs=[blk(ns, S5_GROUP), blk(ns, S5_GROUP), blk(LANES, p), blk(LANES, p),
                  blk(2, ns), blk(ns, 2)],
        out_specs=(blk(width, width), blk(width, 2 * ns), blk(2 * ns, width),
                   blk(1, 2 * ns)),
        compiler_params=_params(("parallel",)),
        name="s5prep",
    )(bb_re, bb_im, c_re.reshape(nlb, LANES, p), c_im.reshape(nlb, LANES, p), lrow, lcol)
    dtile = jnp.tile(d_skip.reshape(nlb, 1, LANES), (1, 1, tc))
    return toep, w_in, w_out, lam, dtile


def _ret_kernel(q_ref, k_ref, v_ref, g_ref, cos_ref, sin_ref, dec_ref, xi_ref,
                zeta_ref, gam_ref, o_ref, state_ref, *, heads):
    c = pl.program_id(1)

    @pl.when(c == 0)
    def _():
        state_ref[...] = jnp.zeros_like(state_ref)

    dqk = q_ref.shape[1] // heads
    dv = v_ref.shape[1] // heads
    half = dqk // 2
    cos = cos_ref[...]
    sin = sin_ref[...]

    def rot(t):
        t1, t2 = t[:, :half], t[:, half:]
        return jnp.concatenate([t1 * cos - t2 * sin, t1 * sin + t2 * cos], axis=-1)

    for h in range(heads):
        q = rot(q_ref[:, h * dqk:(h + 1) * dqk].astype(F32))
        k = rot(k_ref[:, h * dqk:(h + 1) * dqk].astype(F32)) * (dqk ** -0.5)
        v = v_ref[:, h * dv:(h + 1) * dv]
        qb = q.astype(BF16)
        scores = _dot_nt(qb, k.astype(BF16)) * dec_ref[h]
        st = state_ref[h]
        out = _dot(scores.astype(BF16), v) + _dot(qb, st.astype(BF16)) * xi_ref[h]
        kz = (k * zeta_ref[h]).astype(BF16)
        state_ref[h] = st * gam_ref[h] + _dot_tn(kz, v)
        mean = jnp.mean(out, axis=-1, keepdims=True)
        cen = out - mean
        var = jnp.mean(cen * cen, axis=-1, keepdims=True)
        normed = cen * lax.rsqrt(var + GN_EPS)
        g = g_ref[:, h * dv:(h + 1) * dv].astype(F32)
        o_ref[:, h * dv:(h + 1) * dv] = (g * jax.nn.sigmoid(g) * normed).astype(BF16)


def _retention(proj, d, batch, seq):
    t = proj.shape[0]
    heads = RET_HEADS
    ch = min(RET_KERNEL_CHUNK, seq)
    nc = seq // ch
    dqk = d // 2 // heads
    dv = d // heads
    half = dqk // 2
    inv = ROPE_BASE ** (-jnp.arange(0, dqk, 2, dtype=F32) / dqk)
    ang = jnp.arange(seq, dtype=F32)[:, None] * inv[None, :]
    cos, sin = jnp.cos(ang), jnp.sin(ang)
    log_gamma = jnp.log(1.0 - 2.0 ** (-5.0 - jnp.arange(heads, dtype=F32)))
    idx = jnp.arange(ch, dtype=F32)
    rel = idx[:, None] - idx[None, :]
    dec = jnp.where(rel[None] >= 0,
                    jnp.exp(jnp.maximum(rel, 0.0)[None] * log_gamma[:, None, None]), 0.0)
    xi = jnp.exp((idx + 1.0)[None, :] * log_gamma[:, None])[:, :, None]
    zeta = jnp.exp((ch - 1.0 - idx)[None, :] * log_gamma[:, None])[:, :, None]
    gam = jnp.exp(ch * log_gamma)[:, None, None]
    row = lambda b, c: b * nc + c
    return pl.pallas_call(
        functools.partial(_ret_kernel, heads=heads),
        out_shape=jax.ShapeDtypeStruct((t, d), BF16),
        grid=(batch, nc),
        in_specs=[
            pl.BlockSpec((ch, d // 2), lambda b, c: (row(b, c), 0)),
            pl.BlockSpec((ch, d // 2), lambda b, c: (row(b, c), 1)),
            pl.BlockSpec((ch, d), lambda b, c: (row(b, c), 1)),
            pl.BlockSpec((ch, d), lambda b, c: (row(b, c), 2)),
            pl.BlockSpec((ch, half), lambda b, c: (c, 0)),
            pl.BlockSpec((ch, half), lambda b, c: (c, 0)),
            pl.BlockSpec((heads, ch, ch), lambda b, c: (0, 0, 0)),
            pl.BlockSpec((heads, ch, 1), lambda b, c: (0, 0, 0)),
            pl.BlockSpec((heads, ch, 1), lambda b, c: (0, 0, 0)),
            pl.BlockSpec((heads, 1, 1), lambda b, c: (0, 0, 0)),
        ],
        out_specs=pl.BlockSpec((ch, d), lambda b, c: (row(b, c), 0)),
        scratch_shapes=[pltpu.VMEM((heads, dqk, dv), F32)],
        compiler_params=_params(("parallel", "arbitrary")),
        name="retention",
    )(proj, proj, proj, proj, cos, sin, dec, xi, zeta, gam)


def _mix_kernel(*refs, cn, ncast):
    og_ref, yg_ref, ga_ref, gb_ref, h_ref, wo_ref, wv_ref, wg_ref, wout_ref = refs[:9]
    cast_in = refs[9:9 + ncast]
    o_ref = refs[9 + ncast]
    cast_out = refs[10 + ncast:]
    _run_casts(cast_in, cast_out)
    d = h_ref.shape[1]
    og = og_ref[...]
    yg = jnp.concatenate([yg_ref[k] for k in range(yg_ref.shape[0])], axis=-1).astype(BF16)
    merged = []
    for j in range(d // cn):
        cols = slice(j * cn, (j + 1) * cn)
        y_b = _dot(og, wo_ref[:, cols])
        y_a = _dot(yg, wv_ref[:, cols]) * jax.nn.sigmoid(_dot(yg, wg_ref[:, cols]))
        merged.append((jax.nn.sigmoid(ga_ref[:, cols].astype(F32)) * y_a
                       + jax.nn.sigmoid(gb_ref[:, cols].astype(F32)) * y_b).astype(BF16))
    o_ref[...] = h_ref[...] + _dot(jnp.concatenate(merged, axis=-1), wout_ref[...])


def _mix(og, yg, proj, h, w_o, w_v, w_g, w_out, casts=(), *, tm=256, cn=512):
    t, d = h.shape
    nlb = yg.shape[0]
    sw = nlb * LANES
    tm, cn = min(tm, t), min(cn, d)
    cast_specs = [pl.BlockSpec(*_cast_block(c.shape, t // tm, 1)) for c in casts]
    return pl.pallas_call(
        functools.partial(_mix_kernel, cn=cn, ncast=len(casts)),
        out_shape=[jax.ShapeDtypeStruct((t, d), F32)]
        + [jax.ShapeDtypeStruct(c.shape, BF16) for c in casts],
        grid=(t // tm, 1),
        in_specs=[
            pl.BlockSpec((tm, d), lambda i, j: (i, 0)),
            pl.BlockSpec((nlb, tm, LANES), lambda i, j: (0, i, 0)),
            pl.BlockSpec((tm, d), lambda i, j: (i, 3)),
            pl.BlockSpec((tm, d), lambda i, j: (i, 4)),
            pl.BlockSpec((tm, d), lambda i, j: (i, 0)),
            _resident((d, d)), _resident((sw, d)), _resident((sw, d)), _resident((d, d)),
        ] + cast_specs,
        out_specs=[pl.BlockSpec((tm, d), lambda i, j: (i, 0))] + cast_specs,
        compiler_params=_params(("arbitrary", "arbitrary")),
        name="mix",
    )(og, yg, proj, proj, h, w_o, w_v, w_g, w_out, *casts)


def _xattn_kernel(h_ref, g_ref, wq_ref, k_ref, v_ref, wo_ref, o_ref, *, heads):
    h = h_ref[...]
    d = h.shape[1]
    dh = d // heads
    xn = _rms(h, g_ref[...]).astype(BF16)
    outs = []
    for hd in range(heads):
        cols = slice(hd * dh, (hd + 1) * dh)
        q = _dot(xn, wq_ref[:, cols])
        s = _dot_nt(q.astype(BF16), k_ref[0, :, cols]) * (dh ** -0.5)
        e = jnp.exp(s - jnp.max(s, axis=-1, keepdims=True))
        p = e / jnp.sum(e, axis=-1, keepdims=True)
        outs.append(_dot(p.astype(BF16), v_ref[0, :, cols]).astype(BF16))
    o_ref[...] = h + _dot(jnp.concatenate(outs, axis=-1), wo_ref[...])


def _xattn(h, gain, wq, kmem, vmem, wo, batch, *, tm=1024):
    t, d = h.shape
    seq = t // batch
    mlen = kmem.shape[1]
    tm = min(tm, seq)
    nt = seq // tm
    return pl.pallas_call(
        functools.partial(_xattn_kernel, heads=XATTN_HEADS),
        out_shape=jax.ShapeDtypeStruct((t, d), F32),
        grid=(batch, nt),
        in_specs=[
            pl.BlockSpec((tm, d), lambda b, i: (b * nt + i, 0)),
            _resident((1, d)),
            _resident((d, d)),
            pl.BlockSpec((1, mlen, d), lambda b, i: (b, 0, 0)),
            pl.BlockSpec((1, mlen, d), lambda b, i: (b, 0, 0)),
            _resident((d, d)),
        ],
        out_specs=pl.BlockSpec((tm, d), lambda b, i: (b * nt + i, 0)),
        compiler_params=_params(("arbitrary", "arbitrary")),
        name="xattn",
    )(h, gain.reshape(1, d), wq, kmem, vmem, wo)


def kernel(x, mem, ffn1_norm, ffn1_w1, ffn1_w3, ffn1_w2, mix_norm, w_in, s5_a_re, s5_a_im, s5_log_dt, s5_b_re, s5_b_im, s5_c_re, s5_c_im, s5_d, s5_glu_v, s5_glu_g, ret_w_o, w_out, xattn_norm, mem_norm, xattn_wq, xattn_wk, xattn_wv, xattn_wo, ffn2_norm, ffn2_w1, ffn2_w3, ffn2_w2, final_norm):
    batch, seq, d = x.shape
    mlen = mem.shape[1]
    depth = ffn1_w1.shape[0]
    t = batch * seq
    s5_width = s5_d.shape[1]
    bf = lambda w: w.astype(BF16)

    h = x.reshape(t, d)
    mem2 = mem.reshape(batch * mlen, d)
    for l in range(depth):
        last = l == depth - 1
        later = (w_in, s5_glu_v, s5_glu_g, ret_w_o, w_out, xattn_wq, xattn_wk, xattn_wv, xattn_wo)
        h, c_in, c_gv, c_gg, c_ro, c_out, c_wq, c_wk, c_wv, c_wo = _ffn(
            h, ffn1_norm[l], bf(ffn1_w1[l]), bf(ffn1_w3[l]), bf(ffn1_w2[l]),
            casts=[w[l] for w in later])
        proj, us5 = _inproj(h, mix_norm[l], c_in, s5_width)
        toep, s_in, s_out, lam, dtile = _s5_weights(
            s5_a_re[l], s5_a_im[l], s5_log_dt[l], s5_b_re[l], s5_b_im[l],
            s5_c_re[l], s5_c_im[l], s5_d[l])
        yg = _s5(us5, toep, s_in, s_out, lam, dtile, batch)
        og = _retention(proj, d, batch, seq)
        h, c_f1, c_f3, c_f2 = _mix(og, yg, proj, h, c_ro, c_gv, c_gg, c_out,
                                   casts=[ffn2_w1[l], ffn2_w3[l], ffn2_w2[l]])
        kmem, vmem = _normmm(mem2, mem_norm[l], c_wk, c_wv)
        h = _xattn(h, xattn_norm[l], c_wq, kmem.reshape(batch, mlen, d),
                   vmem.reshape(batch, mlen, d), c_wo, batch)
        h, = _ffn(h, ffn2_norm[l], c_f1, c_f3, c_f2, final_norm if last else None)
    if depth == 0:
        raise ValueError("depth must be >= 1")
    return h.reshape(batch, seq, d)
```

```python
import functools

import jax
import jax.numpy as jnp
from jax import lax
from jax.experimental import pallas as pl
from jax.experimental.pallas import tpu as pltpu

F32 = jnp.float32
BF16 = jnp.bfloat16

RMS_EPS = 1e-6
GN_EPS = 1e-5
ROPE_BASE = 10000.0
S5_GROUP = 16
RET_HEADS = 4
XATTN_HEADS = 4
RET_KERNEL_CHUNK = 256
S5_CHUNK = 8
LANES = 128
GROUPS_PER_LANE_BLOCK = LANES // S5_GROUP
VMEM_LIMIT_BYTES = 60 * 1024 * 1024
MXU_TILE = 256


def _params(semantics):
    return pltpu.CompilerParams(dimension_semantics=semantics,
                                vmem_limit_bytes=VMEM_LIMIT_BYTES)


def _resident(shape):
    return pl.BlockSpec(shape, lambda *_: (0,) * len(shape), pipeline_mode=pl.Buffered(1))


def _rms(x, gain):
    ms = jnp.mean(x * x, axis=-1, keepdims=True)
    return x * lax.rsqrt(ms + RMS_EPS) * gain


def _dot(a, b):
    return jnp.dot(a, b, preferred_element_type=F32)


def _dot_nt(a, b):
    return lax.dot_general(a, b, (((1,), (1,)), ((), ())), preferred_element_type=F32)


def _dot_tn(a, b):
    return lax.dot_general(a, b, (((0,), (0,)), ((), ())), preferred_element_type=F32)


def _run_casts(cast_in, cast_out):
    for src_ref, dst_ref in zip(cast_in, cast_out):
        dst_ref[...] = src_ref[...].astype(BF16)


def _ffn_kernel(*refs, final, ncast):
    nin = 6 if final else 5
    x_ref, g_ref, w1_ref, w3_ref, w2_ref = refs[:5]
    fg_ref = refs[5] if final else None
    cast_in = refs[nin:nin + ncast]
    o_ref = refs[nin + ncast]
    cast_out = refs[nin + ncast + 1:nin + 2 * ncast + 1]
    xn_ref = refs[-1]
    j = pl.program_id(1)

    _run_casts(cast_in, cast_out)

    @pl.when(j == 0)
    def _():
        x = x_ref[...]
        xn_ref[...] = _rms(x, g_ref[...]).astype(BF16)
        o_ref[...] = x

    xn = xn_ref[...]
    a = _dot(xn, w1_ref[...])
    b = _dot(xn, w3_ref[...])
    mid = (a * jax.nn.sigmoid(a) * b * 0.5).astype(BF16)
    o_ref[...] += _dot(mid, w2_ref[...])

    if final:
        @pl.when(j == pl.num_programs(1) - 1)
        def _():
            o_ref[...] = _rms(o_ref[...], fg_ref[...])


def _first_step_only(nt):
    return lambda i, j: (jnp.minimum(i + jnp.minimum(j, 1), nt - 1), 0)


def _cast_block(shape, nt, nf):
    r, c = shape

    def tiled(rows, cols):
        return rows % 16 == 0 and cols % LANES == 0

    if r % nt == 0 and c % nf == 0 and tiled(r // nt, c // nf):
        return (r // nt, c // nf), lambda i, j: (i, j)
    if r % nf == 0 and c % nt == 0 and tiled(r // nf, c // nt):
        return (r // nf, c // nt), lambda i, j: (j, i)
    nrow = max(n for n in range(1, nt + 1) if r % n == 0 and (r // n) % 16 == 0)
    ncol = max(n for n in range(1, nf + 1) if c % n == 0 and (c // n) % LANES == 0)
    return ((r // nrow, c // ncol),
            lambda i, j: (jnp.minimum(i, nrow - 1), jnp.minimum(j, ncol - 1)))


def _ffn(x, gain, w1, w3, w2, final_gain=None, casts=(), *, tm=1024, tf=512):
    t, d = x.shape
    f = w1.shape[1]
    tm, tf = min(tm, t), min(tf, f)
    final = final_gain is not None
    in_specs = [
        pl.BlockSpec((tm, d), _first_step_only(t // tm)),
        pl.BlockSpec((1, d), lambda i, j: (0, 0)),
        pl.BlockSpec((d, tf), lambda i, j: (0, j)),
        pl.BlockSpec((d, tf), lambda i, j: (0, j)),
        pl.BlockSpec((tf, d), lambda i, j: (j, 0)),
    ]
    args = [x, gain.reshape(1, d), w1, w3, w2]
    if final:
        in_specs.append(pl.BlockSpec((1, d), lambda i, j: (0, 0)))
        args.append(final_gain.reshape(1, d))
    cast_specs = [pl.BlockSpec(*_cast_block(w.shape, t // tm, f // tf)) for w in casts]
    outs = pl.pallas_call(
        functools.partial(_ffn_kernel, final=final, ncast=len(casts)),
        out_shape=[jax.ShapeDtypeStruct((t, d), F32)]
        + [jax.ShapeDtypeStruct(w.shape, BF16) for w in casts],
        grid=(t // tm, f // tf),
        in_specs=in_specs + cast_specs,
        out_specs=[pl.BlockSpec((tm, d), lambda i, j: (i, 0))] + cast_specs,
        scratch_shapes=[pltpu.VMEM((tm, d), BF16)],
        compiler_params=_params(("arbitrary", "arbitrary")),
        name="ffn_final" if final else "ffn",
    )(*args, *casts)
    return outs


def _inproj_kernel(x_ref, g_ref, wa_ref, wb_ref, proj_ref, us5_ref, xn_ref):
    j = pl.program_id(1)
    tn = wa_ref.shape[1]

    @pl.when(j == 0)
    def _():
        xn = _rms(x_ref[...], g_ref[...]).astype(BF16)
        xn_ref[...] = xn
        res = _dot(xn, wa_ref[...])
        for k in range(us5_ref.shape[0]):
            us5_ref[k] = res[:, k * LANES:(k + 1) * LANES]

    @pl.when(j > 0)
    def _():
        xn = xn_ref[...]
        proj_ref[:, :tn] = _dot(xn, wa_ref[...]).astype(BF16)
        proj_ref[:, tn:] = _dot(xn, wb_ref[...]).astype(BF16)


def _inproj(x, gain, w, s5_width, *, tm=1024):
    t, d = x.shape
    n = w.shape[1]
    tn = s5_width
    tm = min(tm, t)
    pairs, odd = divmod(n // tn - 1, 2)
    assert odd == 0, (n, tn)
    nt, nf = t // tm, 1 + pairs
    return pl.pallas_call(
        _inproj_kernel,
        out_shape=[jax.ShapeDtypeStruct((t, n - tn), BF16),
                   jax.ShapeDtypeStruct((tn // LANES, t, LANES), F32)],
        grid=(nt, nf),
        in_specs=[
            pl.BlockSpec((tm, d), _first_step_only(nt)),
            pl.BlockSpec((1, d), lambda i, j: (0, 0)),
            pl.BlockSpec((d, tn), lambda i, j: (0, jnp.maximum(2 * j - 1, 0))),
            pl.BlockSpec((d, tn), lambda i, j: (0, 2 * jnp.maximum(j, 1))),
        ],
        out_specs=[
            pl.BlockSpec((tm, 2 * tn), lambda i, j: (i, jnp.maximum(j - 1, 0))),
            pl.BlockSpec((tn // LANES, tm, LANES), lambda i, j: (0, i, 0)),
        ],
        scratch_shapes=[pltpu.VMEM((tm, d), BF16)],
        compiler_params=_params(("arbitrary", "arbitrary")),
        name="inproj",
    )(x, gain.reshape(1, d), w, w)


def _normmm_kernel(x_ref, g_ref, wk_ref, wv_ref, k_ref, v_ref, xn_ref):
    @pl.when(pl.program_id(0) == 0)
    def _():
        xn_ref[...] = _rms(x_ref[...], g_ref[...]).astype(BF16)

    xn = xn_ref[...]
    k_ref[...] = _dot(xn, wk_ref[...]).astype(BF16)
    v_ref[...] = _dot(xn, wv_ref[...]).astype(BF16)


def _normmm(x, gain, wk, wv, *, tn=1024):
    t, d = x.shape
    n = wk.shape[1]
    tn = min(tn, n)
    col = lambda rows: pl.BlockSpec((rows, tn), lambda j: (0, j))
    return pl.pallas_call(
        _normmm_kernel,
        out_shape=[jax.ShapeDtypeStruct((t, n), BF16)] * 2,
        grid=(n // tn,),
        in_specs=[_resident((t, d)), _resident((1, d)), col(d), col(d)],
        out_specs=[col(t), col(t)],
        scratch_shapes=[pltpu.VMEM((t, d), BF16)],
        compiler_params=_params(("arbitrary",)),
        name="normmm",
    )(x, gain.reshape(1, d), wk, wv)


def _s5_kernel(u_ref, m_ref, win_ref, wout_ref, lam_ref, d_ref, y_ref, z_ref, s_ref, *, tc, nb):
    rb = z_ref.shape[0] // nb
    seq = rb * tc
    u32 = jnp.concatenate(
        [jnp.concatenate([u_ref[pl.ds(b * seq + j, rb, stride=tc), :] for j in range(tc)], axis=-1)
         for b in range(nb)], axis=0)
    u = u32.astype(BF16)
    z_ref[...] = _dot(u, win_ref[0])
    ns = z_ref.shape[1] // 2
    lam_re = lam_ref[0, :, :ns]
    lam_im = lam_ref[0, :, ns:]

    def step(c, carry):
        out = []
        for b in range(nb):
            s_re, s_im = carry[2 * b], carry[2 * b + 1]
            row = pl.ds(b * rb + c, 1)
            s_ref[row, :ns] = s_re
            s_ref[row, ns:] = s_im
            out += [lam_re * s_re - lam_im * s_im + z_ref[row, :ns],
                    lam_re * s_im + lam_im * s_re + z_ref[row, ns:]]
        return tuple(out)

    zero = jnp.zeros((1, ns), F32)
    lax.fori_loop(0, rb, step, (zero,) * (2 * nb), unroll=8)

    s = s_ref[...].astype(BF16)
    per_tile = MXU_TILE // LANES
    for cb in range(tc // per_tile):
        cols = slice(cb * MXU_TILE, (cb + 1) * MXU_TILE)
        kmax = (cb + 1) * MXU_TILE
        y = (_dot(u[:, :kmax], m_ref[0, :kmax, cols]) + _dot(s, wout_ref[0, :, cols])
             + d_ref[0, :, cols] * u32[:, cols])
        y = jax.nn.gelu(y)
        for b in range(nb):
            for i in range(per_tile):
                y_ref[pl.ds(b * seq + cb * per_tile + i, rb, stride=tc), :] = (
                    y[b * rb:(b + 1) * rb, i * LANES:(i + 1) * LANES])


def _s5(u, m, w_in, w_out, lam, dtile, batch, *, nb=1):
    nlb, t, _ = u.shape
    seq = t // batch
    tc = S5_CHUNK
    _, width, nstate = w_in.shape
    nb = nb if batch % nb == 0 else 1
    rows = nb * seq // tc
    return pl.pallas_call(
        functools.partial(_s5_kernel, tc=tc, nb=nb),
        out_shape=jax.ShapeDtypeStruct((nlb, t, LANES), F32),
        grid=(nlb, batch // nb),
        in_specs=[
            pl.BlockSpec((None, nb * seq, LANES), lambda k, b: (k, b, 0)),
            pl.BlockSpec((1, width, width), lambda k, b: (k, 0, 0)),
            pl.BlockSpec((1, width, nstate), lambda k, b: (k, 0, 0)),
            pl.BlockSpec((1, nstate, width), lambda k, b: (k, 0, 0)),
            pl.BlockSpec((1, 1, nstate), lambda k, b: (k, 0, 0)),
            pl.BlockSpec((1, 1, width), lambda k, b: (k, 0, 0)),
        ],
        out_specs=pl.BlockSpec((None, nb * seq, LANES), lambda k, b: (k, b, 0)),
        scratch_shapes=[pltpu.VMEM((rows, nstate), F32), pltpu.VMEM((rows, nstate), F32)],
        compiler_params=_params(("parallel", "arbitrary")),
        name="s5",
    )(u, m, w_in, w_out, lam, dtile)


def _log2(n):
    assert n & (n - 1) == 0, n
    return n.bit_length() - 1


def _s5prep_kernel(bre_ref, bim_ref, cre_ref, cim_ref, lrow_ref, lcol_ref,
                   toep_ref, win_ref, wout_ref, lam_ref, *, tc):
    ns, gi = bre_ref.shape[1], bre_ref.shape[2]
    lanes, p = cre_ref.shape[1], cre_ref.shape[2]

    def iota(shape, dim):
        return lax.broadcasted_iota(jnp.int32, shape, dim)

    def grp(x, size):
        return lax.shift_right_logical(x, _log2(size))

    def split(a):
        hi = a.astype(BF16)
        return hi, (a - hi.astype(F32)).astype(BF16)

    def dot3(a, b):
        return _dot(a[0], b[0]) + _dot(a[0], b[1]) + _dot(a[1], b[0])

    e_i = ((iota((gi, lanes), 1) & (gi - 1)) == iota((gi, lanes), 0)).astype(BF16)
    e_p = ((iota((p, ns), 1) & (p - 1)) == iota((p, ns), 0)).astype(BF16)

    def repeat(a, e):
        hi, lo = split(a)
        return _dot(hi, e) + _dot(lo, e)

    mask_b = grp(iota((ns, lanes), 0), p) == grp(iota((ns, lanes), 1), gi)
    mask_c = grp(iota((lanes, ns), 0), gi) == grp(iota((lanes, ns), 1), p)

    def bdiag_b(ref):
        return jnp.where(mask_b, repeat(ref[0], e_i), 0.0).T

    def bdiag_c(ref):
        return jnp.where(mask_c, repeat(ref[0], e_p), 0.0).T

    b_re, b_im = bdiag_b(bre_ref), bdiag_b(bim_ref)
    c_re, c_im = bdiag_c(cre_ref), bdiag_c(cim_ref)
    c_re2, c_im2 = split(c_re), split(c_im)
    l_re, l_im = lrow_ref[0, 0:1, :], lrow_ref[0, 1:2, :]
    lc_re, lc_im = lcol_ref[0, :, 0:1], lcol_ref[0, :, 1:2]
    p_re, p_im = jnp.ones_like(l_re), jnp.zeros_like(l_im)
    q_re, q_im = jnp.ones_like(lc_re), jnp.zeros_like(lc_im)
    zero_tile = jnp.zeros((lanes, lanes), BF16)

    def tile(i):
        return slice(i * lanes, (i + 1) * lanes)

    for n in range(tc):
        lb_re = b_re * p_re - b_im * p_im
        lb_im = b_re * p_im + b_im * p_re
        win_ref[0, tile(tc - 1 - n), :ns] = lb_re.astype(BF16)
        win_ref[0, tile(tc - 1 - n), ns:] = lb_im.astype(BF16)
        kern = (dot3(split(lb_re), c_re2) - dot3(split(lb_im), c_im2)).astype(BF16)
        for j in range(tc - n):
            toep_ref[0, tile(j), tile(j + n)] = kern
        if n:
            for t in range(tc - n):
                toep_ref[0, tile(t + n), tile(t)] = zero_tile
        p_re, p_im = p_re * l_re - p_im * l_im, p_re * l_im + p_im * l_re
        q_re, q_im = q_re * lc_re - q_im * lc_im, q_re * lc_im + q_im * lc_re
        wout_ref[0, :ns, tile(n)] = (c_re * q_re - c_im * q_im).astype(BF16)
        wout_ref[0, ns:, tile(n)] = (-(c_re * q_im + c_im * q_re)).astype(BF16)
    lam_ref[0, :, :ns] = p_re
    lam_ref[0, :, ns:] = p_im


def _s5_weights(a_re, a_im, log_dt, b_re, b_im, c_re, c_im, d_skip):
    tc = S5_CHUNK
    g, p = a_re.shape
    gl = GROUPS_PER_LANE_BLOCK
    nlb = g // gl
    ns = gl * p
    dt = jnp.exp(log_dt)[:, None]
    mag = jnp.exp(a_re * dt)
    l_re = mag * jnp.cos(a_im * dt)
    l_im = mag * jnp.sin(a_im * dt)
    den = a_re * a_re + a_im * a_im
    n_re = l_re - 1.0
    n_im = l_im
    f_re = (n_re * a_re + n_im * a_im) / den
    f_im = (n_im * a_re - n_re * a_im) / den
    bb_re = (f_re[..., None] * b_re - f_im[..., None] * b_im).reshape(nlb, ns, S5_GROUP)
    bb_im = (f_re[..., None] * b_im + f_im[..., None] * b_re).reshape(nlb, ns, S5_GROUP)
    lrow = jnp.stack([l_re.reshape(nlb, ns), l_im.reshape(nlb, ns)], axis=1)
    lcol = jnp.stack([l_re.reshape(nlb, ns), l_im.reshape(nlb, ns)], axis=2)
    width = tc * LANES
    blk = lambda *shape: pl.BlockSpec((1,) + shape, lambda k: (k, 0, 0))
    toep, w_in, w_out, lam = pl.pallas_call(
        functools.partial(_s5prep_kernel, tc=tc),
        out_shape=(jax.ShapeDtypeStruct((nlb, width, width), BF16),
                   jax.ShapeDtypeStruct((nlb, width, 2 * ns), BF16),
                   jax.ShapeDtypeStruct((nlb, 2 * ns, width), BF16),
                   jax.ShapeDtypeStruct((nlb, 1, 2 * ns), F32)),
        grid=(nlb,),
        in_specs=[blk(ns, S5_GROUP), blk(ns, S5_GROUP), blk(LANES, p), blk(LANES, p),
                  blk(2, ns), blk(ns, 2)],
        out_specs=(blk(width, width), blk(width, 2 * ns), blk(2 * ns, width),
                   blk(1, 2 * ns)),
        compiler_params=_params(("parallel",)),
        name="s5prep",
    )(bb_re, bb_im, c_re.reshape(nlb, LANES, p), c_im.reshape(nlb, LANES, p), lrow, lcol)
    dtile = jnp.tile(d_skip.reshape(nlb, 1, LANES), (1, 1, tc))
    return toep, w_in, w_out, lam, dtile


def _ret_kernel(q_ref, k_ref, v_ref, g_ref, cos_ref, sin_ref, dec_ref, xi_ref,
                zeta_ref, gam_ref, o_ref, state_ref, *, heads):
    c = pl.program_id(1)

    @pl.when(c == 0)
    def _():
        state_ref[...] = jnp.zeros_like(state_ref)

    dqk = q_ref.shape[1] // heads
    dv = v_ref.shape[1] // heads
    half = dqk // 2
    cos = cos_ref[...]
    sin = sin_ref[...]

    def rot(t):
        t1, t2 = t[:, :half], t[:, half:]
        return jnp.concatenate([t1 * cos - t2 * sin, t1 * sin + t2 * cos], axis=-1)

    for h in range(heads):
        q = rot(q_ref[:, h * dqk:(h + 1) * dqk].astype(F32))
        k = rot(k_ref[:, h * dqk:(h + 1) * dqk].astype(F32)) * (dqk ** -0.5)
        v = v_ref[:, h * dv:(h + 1) * dv]
        qb = q.astype(BF16)
        scores = _dot_nt(qb, k.astype(BF16)) * dec_ref[h]
        st = state_ref[h]
        out = _dot(scores.astype(BF16), v) + _dot(qb, st.astype(BF16)) * xi_ref[h]
        kz = (k * zeta_ref[h]).astype(BF16)
        state_ref[h] = st * gam_ref[h] + _dot_tn(kz, v)
        mean = jnp.mean(out, axis=-1, keepdims=True)
        cen = out - mean
        var = jnp.mean(cen * cen, axis=-1, keepdims=True)
        normed = cen * lax.rsqrt(var + GN_EPS)
        g = g_ref[:, h * dv:(h + 1) * dv].astype(F32)
        o_ref[:, h * dv:(h + 1) * dv] = (g * jax.nn.sigmoid(g) * normed).astype(BF16)


def _retention(proj, d, batch, seq):
    t = proj.shape[0]
    heads = RET_HEADS
    ch = min(RET_KERNEL_CHUNK, seq)
    nc = seq // ch
    dqk = d // 2 // heads
    dv = d // heads
    half = dqk // 2
    inv = ROPE_BASE ** (-jnp.arange(0, dqk, 2, dtype=F32) / dqk)
    ang = jnp.arange(seq, dtype=F32)[:, None] * inv[None, :]
    cos, sin = jnp.cos(ang), jnp.sin(ang)
    log_gamma = jnp.log(1.0 - 2.0 ** (-5.0 - jnp.arange(heads, dtype=F32)))
    idx = jnp.arange(ch, dtype=F32)
    rel = idx[:, None] - idx[None, :]
    dec = jnp.where(rel[None] >= 0,
                    jnp.exp(jnp.maximum(rel, 0.0)[None] * log_gamma[:, None, None]), 0.0)
    xi = jnp.exp((idx + 1.0)[None, :] * log_gamma[:, None])[:, :, None]
    zeta = jnp.exp((ch - 1.0 - idx)[None, :] * log_gamma[:, None])[:, :, None]
    gam = jnp.exp(ch * log_gamma)[:, None, None]
    row = lambda b, c: b * nc + c
    return pl.pallas_call(
        functools.partial(_ret_kernel, heads=heads),
        out_shape=jax.ShapeDtypeStruct((t, d), BF16),
        grid=(batch, nc),
        in_specs=[
            pl.BlockSpec((ch, d // 2), lambda b, c: (row(b, c), 0)),
            pl.BlockSpec((ch, d // 2), lambda b, c: (row(b, c), 1)),
            pl.BlockSpec((ch, d), lambda b, c: (row(b, c), 1)),
            pl.BlockSpec((ch, d), lambda b, c: (row(b, c), 2)),
            pl.BlockSpec((ch, half), lambda b, c: (c, 0)),
            pl.BlockSpec((ch, half), lambda b, c: (c, 0)),
            pl.BlockSpec((heads, ch, ch), lambda b, c: (0, 0, 0)),
            pl.BlockSpec((heads, ch, 1), lambda b, c: (0, 0, 0)),
            pl.BlockSpec((heads, ch, 1), lambda b, c: (0, 0, 0)),
            pl.BlockSpec((heads, 1, 1), lambda b, c: (0, 0, 0)),
        ],
        out_specs=pl.BlockSpec((ch, d), lambda b, c: (row(b, c), 0)),
        scratch_shapes=[pltpu.VMEM((heads, dqk, dv), F32)],
        compiler_params=_params(("parallel", "arbitrary")),
        name="retention",
    )(proj, proj, proj, proj, cos, sin, dec, xi, zeta, gam)


def _mix_kernel(*refs, cn, ncast):
    og_ref, yg_ref, ga_ref, gb_ref, h_ref, wo_ref, wv_ref, wg_ref, wout_ref = refs[:9]
    cast_in = refs[9:9 + ncast]
    o_ref = refs[9 + ncast]
    cast_out = refs[10 + ncast:]
    _run_casts(cast_in, cast_out)
    d = h_ref.shape[1]
    og = og_ref[...]
    yg = jnp.concatenate([yg_ref[k] for k in range(yg_ref.shape[0])], axis=-1).astype(BF16)
    merged = []
    for j in range(d // cn):
        cols = slice(j * cn, (j + 1) * cn)
        y_b = _dot(og, wo_ref[:, cols])
        y_a = _dot(yg, wv_ref[:, cols]) * jax.nn.sigmoid(_dot(yg, wg_ref[:, cols]))
        merged.append((jax.nn.sigmoid(ga_ref[:, cols].astype(F32)) * y_a
                       + jax.nn.sigmoid(gb_ref[:, cols].astype(F32)) * y_b).astype(BF16))
    o_ref[...] = h_ref[...] + _dot(jnp.concatenate(merged, axis=-1), wout_ref[...])


def _mix(og, yg, proj, h, w_o, w_v, w_g, w_out, casts=(), *, tm=256, cn=512):
    t, d = h.shape
    nlb = yg.shape[0]
    sw = nlb * LANES
    tm, cn = min(tm, t), min(cn, d)
    cast_specs = [pl.BlockSpec(*_cast_block(c.shape, t // tm, 1)) for c in casts]
    return pl.pallas_call(
        functools.partial(_mix_kernel, cn=cn, ncast=len(casts)),
        out_shape=[jax.ShapeDtypeStruct((t, d), F32)]
        + [jax.ShapeDtypeStruct(c.shape, BF16) for c in casts],
        grid=(t // tm, 1),
        in_specs=[
            pl.BlockSpec((tm, d), lambda i, j: (i, 0)),
            pl.BlockSpec((nlb, tm, LANES), lambda i, j: (0, i, 0)),
            pl.BlockSpec((tm, d), lambda i, j: (i, 3)),
            pl.BlockSpec((tm, d), lambda i, j: (i, 4)),
            pl.BlockSpec((tm, d), lambda i, j: (i, 0)),
            _resident((d, d)), _resident((sw, d)), _resident((sw, d)), _resident((d, d)),
        ] + cast_specs,
        out_specs=[pl.BlockSpec((tm, d), lambda i, j: (i, 0))] + cast_specs,
        compiler_params=_params(("arbitrary", "arbitrary")),
        name="mix",
    )(og, yg, proj, proj, h, w_o, w_v, w_g, w_out, *casts)


def _xattn_kernel(h_ref, g_ref, wq_ref, k_ref, v_ref, wo_ref, o_ref, *, heads):
    h = h_ref[...]
    d = h.shape[1]
    dh = d // heads
    xn = _rms(h, g_ref[...]).astype(BF16)
    outs = []
    for hd in range(heads):
        cols = slice(hd * dh, (hd + 1) * dh)
        q = _dot(xn, wq_ref[:, cols])
        s = _dot_nt(q.astype(BF16), k_ref[0, :, cols]) * (dh ** -0.5)
        e = jnp.exp(s - jnp.max(s, axis=-1, keepdims=True))
        p = e / jnp.sum(e, axis=-1, keepdims=True)
        outs.append(_dot(p.astype(BF16), v_ref[0, :, cols]).astype(BF16))
    o_ref[...] = h + _dot(jnp.concatenate(outs, axis=-1), wo_ref[...])


def _xattn(h, gain, wq, kmem, vmem, wo, batch, *, tm=1024):
    t, d = h.shape
    seq = t // batch
    mlen = kmem.shape[1]
    tm = min(tm, seq)
    nt = seq // tm
    return pl.pallas_call(
        functools.partial(_xattn_kernel, heads=XATTN_HEADS),
        out_shape=jax.ShapeDtypeStruct((t, d), F32),
        grid=(batch, nt),
        in_specs=[
            pl.BlockSpec((tm, d), lambda b, i: (b * nt + i, 0)),
            _resident((1, d)),
            _resident((d, d)),
            pl.BlockSpec((1, mlen, d), lambda b, i: (b, 0, 0)),
            pl.BlockSpec((1, mlen, d), lambda b, i: (b, 0, 0)),
            _resident((d, d)),
        ],
        out_specs=pl.BlockSpec((tm, d), lambda b, i: (b * nt + i, 0)),
        compiler_params=_params(("arbitrary", "arbitrary")),
        name="xattn",
    )(h, gain.reshape(1, d), wq, kmem, vmem, wo)


def kernel(x, mem, ffn1_norm, ffn1_w1, ffn1_w3, ffn1_w2, mix_norm, w_in, s5_a_re, s5_a_im, s5_log_dt, s5_b_re, s5_b_im, s5_c_re, s5_c_im, s5_d, s5_glu_v, s5_glu_g, ret_w_o, w_out, xattn_norm, mem_norm, xattn_wq, xattn_wk, xattn_wv, xattn_wo, ffn2_norm, ffn2_w1, ffn2_w3, ffn2_w2, final_norm):
    batch, seq, d = x.shape
    mlen = mem.shape[1]
    depth = ffn1_w1.shape[0]
    t = batch * seq
    s5_width = s5_d.shape[1]
    bf = lambda w: w.astype(BF16)

    h = x.reshape(t, d)
    mem2 = mem.reshape(batch * mlen, d)
    for l in range(depth):
        last = l == depth - 1
        later = (w_in, s5_glu_v, s5_glu_g, ret_w_o, w_out, xattn_wq, xattn_wk, xattn_wv, xattn_wo)
        h, c_in, c_gv, c_gg, c_ro, c_out, c_wq, c_wk, c_wv, c_wo = _ffn(
            h, ffn1_norm[l], bf(ffn1_w1[l]), bf(ffn1_w3[l]), bf(ffn1_w2[l]),
            casts=[w[l] for w in later])
        proj, us5 = _inproj(h, mix_norm[l], c_in, s5_width)
        toep, s_in, s_out, lam, dtile = _s5_weights(
            s5_a_re[l], s5_a_im[l], s5_log_dt[l], s5_b_re[l], s5_b_im[l],
            s5_c_re[l], s5_c_im[l], s5_d[l])
        yg = _s5(us5, toep, s_in, s_out, lam, dtile, batch)
        og = _retention(proj, d, batch, seq)
        h, c_f1, c_f3, c_f2 = _mix(og, yg, proj, h, c_ro, c_gv, c_gg, c_out,
                                   casts=[ffn2_w1[l], ffn2_w3[l], ffn2_w2[l]])
        kmem, vmem = _normmm(mem2, mem_norm[l], c_wk, c_wv)
        h = _xattn(h, xattn_norm[l], c_wq, kmem.reshape(batch, mlen, d),
                   vmem.reshape(batch, mlen, d), c_wo, batch)
        h, = _ffn(h, ffn2_norm[l], c_f1, c_f3, c_f2, final_norm if last else None)
    if depth == 0:
        raise ValueError("depth must be >= 1")
    return h.reshape(batch, seq, d)
```

```python
import functools

import jax
import jax.numpy as jnp
from jax import lax
from jax.experimental import pallas as pl
from jax.experimental.pallas import tpu as pltpu

F32 = jnp.float32
BF16 = jnp.bfloat16

RMS_EPS = 1e-6
GN_EPS = 1e-5
ROPE_BASE = 10000.0
S5_GROUP = 16
RET_HEADS = 4
XATTN_HEADS = 4
RET_KERNEL_CHUNK = 256
S5_CHUNK = 8
LANES = 128
GROUPS_PER_LANE_BLOCK = LANES // S5_GROUP
VMEM_LIMIT_BYTES = 60 * 1024 * 1024
MXU_TILE = 256


def _params(semantics):
    return pltpu.CompilerParams(dimension_semantics=semantics,
                                vmem_limit_bytes=VMEM_LIMIT_BYTES)


def _resident(shape):
    return pl.BlockSpec(shape, lambda *_: (0,) * len(shape), pipeline_mode=pl.Buffered(1))


def _rms(x, gain):
    ms = jnp.mean(x * x, axis=-1, keepdims=True)
    return x * lax.rsqrt(ms + RMS_EPS) * gain


def _dot(a, b):
    return jnp.dot(a, b, preferred_element_type=F32)


def _dot_nt(a, b):
    return lax.dot_general(a, b, (((1,), (1,)), ((), ())), preferred_element_type=F32)


def _dot_tn(a, b):
    return lax.dot_general(a, b, (((0,), (0,)), ((), ())), preferred_element_type=F32)


def _run_casts(cast_in, cast_out):
    for src_ref, dst_ref in zip(cast_in, cast_out):
        dst_ref[...] = src_ref[...].astype(BF16)


def _ffn_kernel(*refs, final, ncast):
    nin = 6 if final else 5
    x_ref, g_ref, w1_ref, w3_ref, w2_ref = refs[:5]
    fg_ref = refs[5] if final else None
    cast_in = refs[nin:nin + ncast]
    o_ref = refs[nin + ncast]
    cast_out = refs[nin + ncast + 1:nin + 2 * ncast + 1]
    xn_ref = refs[-1]
    j = pl.program_id(1)

    _run_casts(cast_in, cast_out)

    last = pl.num_programs(1) - 1

    def half_swiglu(xn):
        a = _dot(xn, w1_ref[...])
        b = _dot(xn, w3_ref[...])
        mid = (a * jax.nn.sigmoid(a) * b * 0.5).astype(BF16)
        return _dot(mid, w2_ref[...])

    @pl.when(j == 0)
    def _():
        x = x_ref[...]
        xn = _rms(x, g_ref[...]).astype(BF16)
        xn_ref[...] = xn
        o_ref[...] = x + half_swiglu(xn)

    @pl.when(jnp.logical_and(j > 0, j < last) if final else j > 0)
    def _():
        o_ref[...] += half_swiglu(xn_ref[...])

    if final:
        @pl.when(j == last)
        def _():
            o_ref[...] = _rms(o_ref[...] + half_swiglu(xn_ref[...]), fg_ref[...])


def _first_step_only(nt):
    return lambda i, j: (jnp.minimum(i + jnp.minimum(j, 1), nt - 1), 0)


def _cast_block(shape, nt, nf):
    r, c = shape

    def tiled(rows, cols):
        return rows % 16 == 0 and cols % LANES == 0

    if r % nt == 0 and c % nf == 0 and tiled(r // nt, c // nf):
        return (r // nt, c // nf), lambda i, j: (i, j)
    if r % nf == 0 and c % nt == 0 and tiled(r // nf, c // nt):
        return (r // nf, c // nt), lambda i, j: (j, i)
    nrow = max(n for n in range(1, nt + 1) if r % n == 0 and (r // n) % 16 == 0)
    ncol = max(n for n in range(1, nf + 1) if c % n == 0 and (c // n) % LANES == 0)
    return ((r // nrow, c // ncol),
            lambda i, j: (jnp.minimum(i, nrow - 1), jnp.minimum(j, ncol - 1)))


def _ffn(x, gain, w1, w3, w2, final_gain=None, casts=(), *, tm=1024, tf=512):
    t, d = x.shape
    f = w1.shape[1]
    tm, tf = min(tm, t), min(tf, f)
    final = final_gain is not None
    in_specs = [
        pl.BlockSpec((tm, d), _first_step_only(t // tm)),
        pl.BlockSpec((1, d), lambda i, j: (0, 0)),
        pl.BlockSpec((d, tf), lambda i, j: (0, j)),
        pl.BlockSpec((d, tf), lambda i, j: (0, j)),
        pl.BlockSpec((tf, d), lambda i, j: (j, 0)),
    ]
    args = [x, gain.reshape(1, d), w1, w3, w2]
    if final:
        in_specs.append(pl.BlockSpec((1, d), lambda i, j: (0, 0)))
        args.append(final_gain.reshape(1, d))
    cast_specs = [pl.BlockSpec(*_cast_block(w.shape, t // tm, f // tf)) for w in casts]
    outs = pl.pallas_call(
        functools.partial(_ffn_kernel, final=final, ncast=len(casts)),
        out_shape=[jax.ShapeDtypeStruct((t, d), F32)]
        + [jax.ShapeDtypeStruct(w.shape, BF16) for w in casts],
        grid=(t // tm, f // tf),
        in_specs=in_specs + cast_specs,
        out_specs=[pl.BlockSpec((tm, d), lambda i, j: (i, 0))] + cast_specs,
        scratch_shapes=[pltpu.VMEM((tm, d), BF16)],
        compiler_params=_params(("arbitrary", "arbitrary")),
        name="ffn_final" if final else "ffn",
    )(*args, *casts)
    return outs


def _rotary(t, cos, sin, heads):
    dh = t.shape[1] // heads
    half = dh // 2
    out = []
    for h in range(heads):
        t1, t2 = t[:, h * dh:h * dh + half], t[:, h * dh + half:(h + 1) * dh]
        out += [t1 * cos - t2 * sin, t1 * sin + t2 * cos]
    return jnp.concatenate(out, axis=-1)


def _inproj_kernel(x_ref, g_ref, wa_ref, wb_ref, cos_ref, sin_ref, proj_ref, us5_ref, xn_ref,
                   *, heads):
    j = pl.program_id(1)
    tn = wa_ref.shape[1]

    @pl.when(j == 0)
    def _():
        xn = _rms(x_ref[...], g_ref[...]).astype(BF16)
        xn_ref[...] = xn
        res = _dot(xn, wa_ref[...])
        for k in range(us5_ref.shape[0]):
            us5_ref[k] = res[:, k * LANES:(k + 1) * LANES]

    @pl.when(j == 1)
    def _():
        xn = xn_ref[...]
        cos, sin = cos_ref[...], sin_ref[...]
        q = _rotary(_dot(xn, wa_ref[...]), cos, sin, heads)
        k = _rotary(_dot(xn, wb_ref[...]), cos, sin, heads) * ((tn // heads) ** -0.5)
        proj_ref[:, :tn] = q.astype(BF16)
        proj_ref[:, tn:] = k.astype(BF16)

    @pl.when(j > 1)
    def _():
        xn = xn_ref[...]
        proj_ref[:, :tn] = _dot(xn, wa_ref[...]).astype(BF16)
        proj_ref[:, tn:] = _dot(xn, wb_ref[...]).astype(BF16)


def _rope_tables(seq, dh):
    inv = ROPE_BASE ** (-jnp.arange(0, dh, 2, dtype=F32) / dh)
    ang = jnp.arange(seq, dtype=F32)[:, None] * inv[None, :]
    return jnp.cos(ang), jnp.sin(ang)


def _inproj(x, gain, w, s5_width, seq, *, tm=1024):
    t, d = x.shape
    n = w.shape[1]
    tn = s5_width
    tm = min(tm, seq)
    assert seq % tm == 0 and tn == d // 2, (seq, tm, tn, d)
    pairs, odd = divmod(n // tn - 1, 2)
    assert odd == 0, (n, tn)
    nt, nf = t // tm, 1 + pairs
    half = tn // RET_HEADS // 2
    cos, sin = _rope_tables(seq, 2 * half)
    pos = pl.BlockSpec((tm, half), lambda i, j: (i % (seq // tm), 0))
    return pl.pallas_call(
        functools.partial(_inproj_kernel, heads=RET_HEADS),
        out_shape=[jax.ShapeDtypeStruct((t, n - tn), BF16),
                   jax.ShapeDtypeStruct((tn // LANES, t, LANES), F32)],
        grid=(nt, nf),
        in_specs=[
            pl.BlockSpec((tm, d), _first_step_only(nt)),
            pl.BlockSpec((1, d), lambda i, j: (0, 0)),
            pl.BlockSpec((d, tn), lambda i, j: (0, jnp.maximum(2 * j - 1, 0))),
            pl.BlockSpec((d, tn), lambda i, j: (0, 2 * jnp.maximum(j, 1))),
            pos, pos,
        ],
        out_specs=[
            pl.BlockSpec((tm, 2 * tn), lambda i, j: (i, jnp.maximum(j - 1, 0))),
            pl.BlockSpec((tn // LANES, tm, LANES), lambda i, j: (0, i, 0)),
        ],
        scratch_shapes=[pltpu.VMEM((tm, d), BF16)],
        compiler_params=_params(("arbitrary", "arbitrary")),
        name="inproj",
    )(x, gain.reshape(1, d), w, w, cos, sin)


def _normmm_kernel(x_ref, g_ref, wk_ref, wv_ref, k_ref, v_ref, xn_ref):
    @pl.when(pl.program_id(0) == 0)
    def _():
        xn_ref[...] = _rms(x_ref[...], g_ref[...]).astype(BF16)

    xn = xn_ref[...]
    k_ref[...] = _dot(xn, wk_ref[...]).astype(BF16)
    v_ref[...] = _dot(xn, wv_ref[...]).astype(BF16)


def _normmm(x, gain, wk, wv, *, tn=1024):
    t, d = x.shape
    n = wk.shape[1]
    tn = min(tn, n)
    col = lambda rows: pl.BlockSpec((rows, tn), lambda j: (0, j))
    return pl.pallas_call(
        _normmm_kernel,
        out_shape=[jax.ShapeDtypeStruct((t, n), BF16)] * 2,
        grid=(n // tn,),
        in_specs=[_resident((t, d)), _resident((1, d)), col(d), col(d)],
        out_specs=[col(t), col(t)],
        scratch_shapes=[pltpu.VMEM((t, d), BF16)],
        compiler_params=_params(("arbitrary",)),
        name="normmm",
    )(x, gain.reshape(1, d), wk, wv)


def _s5_kernel(u_ref, m_ref, win_ref, wout_ref, lam_ref, d_ref, y_ref, z_ref, s_ref, *, tc, nb):
    rb = z_ref.shape[0] // nb
    seq = rb * tc
    u32 = jnp.concatenate(
        [jnp.concatenate([u_ref[pl.ds(b * seq + j, rb, stride=tc), :] for j in range(tc)], axis=-1)
         for b in range(nb)], axis=0)
    u = u32.astype(BF16)
    z_ref[...] = _dot(u, win_ref[0])
    ns = z_ref.shape[1] // 2
    lam_re = lam_ref[0, :, :ns]
    lam_im = lam_ref[0, :, ns:]

    def step(c, carry):
        out = []
        for b in range(nb):
            s_re, s_im = carry[2 * b], carry[2 * b + 1]
            row = pl.ds(b * rb + c, 1)
            s_ref[row, :ns] = s_re
            s_ref[row, ns:] = s_im
            out += [lam_re * s_re - lam_im * s_im + z_ref[row, :ns],
                    lam_re * s_im + lam_im * s_re + z_ref[row, ns:]]
        return tuple(out)

    zero = jnp.zeros((1, ns), F32)
    lax.fori_loop(0, rb, step, (zero,) * (2 * nb), unroll=8)

    s = s_ref[...].astype(BF16)
    per_tile = MXU_TILE // LANES
    for cb in range(tc // per_tile):
        cols = slice(cb * MXU_TILE, (cb + 1) * MXU_TILE)
        kmax = (cb + 1) * MXU_TILE
        y = (_dot(u[:, :kmax], m_ref[0, :kmax, cols]) + _dot(s, wout_ref[0, :, cols])
             + d_ref[0, :, cols] * u32[:, cols])
        y = jax.nn.gelu(y)
        for b in range(nb):
            for i in range(per_tile):
                y_ref[pl.ds(b * seq + cb * per_tile + i, rb, stride=tc), :] = (
                    y[b * rb:(b + 1) * rb, i * LANES:(i + 1) * LANES])


def _s5(u, m, w_in, w_out, lam, dtile, batch, *, nb=1):
    nlb, t, _ = u.shape
    seq = t // batch
    tc = S5_CHUNK
    _, width, nstate = w_in.shape
    nb = nb if batch % nb == 0 else 1
    rows = nb * seq // tc
    return pl.pallas_call(
        functools.partial(_s5_kernel, tc=tc, nb=nb),
        out_shape=jax.ShapeDtypeStruct((nlb, t, LANES), F32),
        grid=(nlb, batch // nb),
        in_specs=[
            pl.BlockSpec((None, nb * seq, LANES), lambda k, b: (k, b, 0)),
            pl.BlockSpec((1, width, width), lambda k, b: (k, 0, 0)),
            pl.BlockSpec((1, width, nstate), lambda k, b: (k, 0, 0)),
            pl.BlockSpec((1, nstate, width), lambda k, b: (k, 0, 0)),
            pl.BlockSpec((1, 1, nstate), lambda k, b: (k, 0, 0)),
            pl.BlockSpec((1, 1, width), lambda k, b: (k, 0, 0)),
        ],
        out_specs=pl.BlockSpec((None, nb * seq, LANES), lambda k, b: (k, b, 0)),
        scratch_shapes=[pltpu.VMEM((rows, nstate), F32), pltpu.VMEM((rows, nstate), F32)],
        compiler_params=_params(("parallel", "arbitrary")),
        name="s5",
    )(u, m, w_in, w_out, lam, dtile)


def _log2(n):
    assert n & (n - 1) == 0, n
    return n.bit_length() - 1


def _s5prep_kernel(bre_ref, bim_ref, cre_ref, cim_ref, lrow_ref, lcol_ref,
                   toep_ref, win_ref, wout_ref, lam_ref, *, tc):
    ns, gi = bre_ref.shape[1], bre_ref.shape[2]
    lanes, p = cre_ref.shape[1], cre_ref.shape[2]

    def iota(shape, dim):
        return lax.broadcasted_iota(jnp.int32, shape, dim)

    def grp(x, size):
        return lax.shift_right_logical(x, _log2(size))

    def split(a):
        hi = a.astype(BF16)
        return hi, (a - hi.astype(F32)).astype(BF16)

    def dot3(a, b):
        return _dot(a[0], b[0]) + _dot(a[0], b[1]) + _dot(a[1], b[0])

    e_i = ((iota((gi, lanes), 1) & (gi - 1)) == iota((gi, lanes), 0)).astype(BF16)
    e_p = ((iota((p, ns), 1) & (p - 1)) == iota((p, ns), 0)).astype(BF16)

    def repeat(a, e):
        hi, lo = split(a)
        return _dot(hi, e) + _dot(lo, e)

    mask_b = grp(iota((ns, lanes), 0), p) == grp(iota((ns, lanes), 1), gi)
    mask_c = grp(iota((lanes, ns), 0), gi) == grp(iota((lanes, ns), 1), p)

    def bdiag_b(ref):
        return jnp.where(mask_b, repeat(ref[0], e_i), 0.0).T

    def bdiag_c(ref):
        return jnp.where(mask_c, repeat(ref[0], e_p), 0.0).T

    b_re, b_im = bdiag_b(bre_ref), bdiag_b(bim_ref)
    c_re, c_im = bdiag_c(cre_ref), bdiag_c(cim_ref)
    c_re2, c_im2 = split(c_re), split(c_im)
    l_re, l_im = lrow_ref[0, 0:1, :], lrow_ref[0, 1:2, :]
    lc_re, lc_im = lcol_ref[0, :, 0:1], lcol_ref[0, :, 1:2]
    p_re, p_im = jnp.ones_like(l_re), jnp.zeros_like(l_im)
    q_re, q_im = jnp.ones_like(lc_re), jnp.zeros_like(lc_im)
    zero_tile = jnp.zeros((lanes, lanes), BF16)

    def tile(i):
        return slice(i * lanes, (i + 1) * lanes)

    for n in range(tc):
        lb_re = b_re * p_re - b_im * p_im
        lb_im = b_re * p_im + b_im * p_re
        win_ref[0, tile(tc - 1 - n), :ns] = lb_re.astype(BF16)
        win_ref[0, tile(tc - 1 - n), ns:] = lb_im.astype(BF16)
        kern = (dot3(split(lb_re), c_re2) - dot3(split(lb_im), c_im2)).astype(BF16)
        for j in range(tc - n):
            toep_ref[0, tile(j), tile(j + n)] = kern
        if n:
            for t in range(tc - n):
                toep_ref[0, tile(t + n), tile(t)] = zero_tile
        p_re, p_im = p_re * l_re - p_im * l_im, p_re * l_im + p_im * l_re
        q_re, q_im = q_re * lc_re - q_im * lc_im, q_re * lc_im + q_im * lc_re
        wout_ref[0, :ns, tile(n)] = (c_re * q_re - c_im * q_im).astype(BF16)
        wout_ref[0, ns:, tile(n)] = (-(c_re * q_im + c_im * q_re)).astype(BF16)
    lam_ref[0, :, :ns] = p_re
    lam_ref[0, :, ns:] = p_im


def _s5_weights(a_re, a_im, log_dt, b_re, b_im, c_re, c_im, d_skip):
    tc = S5_CHUNK
    g, p = a_re.shape
    gl = GROUPS_PER_LANE_BLOCK
    nlb = g // gl
    ns = gl * p
    dt = jnp.exp(log_dt)[:, None]
    mag = jnp.exp(a_re * dt)
    l_re = mag * jnp.cos(a_im * dt)
    l_im = mag * jnp.sin(a_im * dt)
    den = a_re * a_re + a_im * a_im
    n_re = l_re - 1.0
    n_im = l_im
    f_re = (n_re * a_re + n_im * a_im) / den
    f_im = (n_im * a_re - n_re * a_im) / den
    bb_re = (f_re[..., None] * b_re - f_im[..., None] * b_im).reshape(nlb, ns, S5_GROUP)
    bb_im = (f_re[..., None] * b_im + f_im[..., None] * b_re).reshape(nlb, ns, S5_GROUP)
    lrow = jnp.stack([l_re.reshape(nlb, ns), l_im.reshape(nlb, ns)], axis=1)
    lcol = jnp.stack([l_re.reshape(nlb, ns), l_im.reshape(nlb, ns)], axis=2)
    width = tc * LANES
    blk = lambda *shape: pl.BlockSpec((1,) + shape, lambda k: (k, 0, 0))
    toep, w_in, w_out, lam = pl.pallas_call(
        functools.partial(_s5prep_kernel, tc=tc),
        out_shape=(jax.ShapeDtypeStruct((nlb, width, width), BF16),
                   jax.ShapeDtypeStruct((nlb, width, 2 * ns), BF16),
                   jax.ShapeDtypeStruct((nlb, 2 * ns, width), BF16),
                   jax.ShapeDtypeStruct((nlb, 1, 2 * ns), F32)),
        grid=(nlb,),
        in_specs=[blk(ns, S5_GROUP), blk(ns, S5_GROUP), blk(LANES, p), blk(LANES, p),
                  blk(2, ns), blk(ns, 2)],
        out_specs=(blk(width, width), blk(width, 2 * ns), blk(2 * ns, width),
                   blk(1, 2 * ns)),
        compiler_params=_params(("parallel",)),
        name="s5prep",
    )(bb_re, bb_im, c_re.reshape(nlb, LANES, p), c_im.reshape(nlb, LANES, p), lrow, lcol)
    dtile = jnp.tile(d_skip.reshape(nlb, 1, LANES), (1, 1, tc))
    return toep, w_in, w_out, lam, dtile


def _ret_kernel(q_ref, k_ref, v_ref, g_ref, dec_ref, xi_ref, zeta_ref, gam_ref, o_ref,
                state_ref, *, heads):
    c = pl.program_id(1)

    @pl.when(c == 0)
    def _():
        state_ref[...] = jnp.zeros_like(state_ref)

    dqk = q_ref.shape[1] // heads
    dv = v_ref.shape[1] // heads
    for h in range(heads):
        qb = q_ref[:, h * dqk:(h + 1) * dqk]
        kb = k_ref[:, h * dqk:(h + 1) * dqk]
        v = v_ref[:, h * dv:(h + 1) * dv]
        scores = _dot_nt(qb, kb) * dec_ref[h]
        st = state_ref[h]
        out = _dot(scores.astype(BF16), v) + _dot(qb, st.astype(BF16)) * xi_ref[h]
        kz = (kb.astype(F32) * zeta_ref[h]).astype(BF16)
        state_ref[h] = st * gam_ref[h] + _dot_tn(kz, v)
        mean = jnp.mean(out, axis=-1, keepdims=True)
        cen = out - mean
        var = jnp.mean(cen * cen, axis=-1, keepdims=True)
        normed = cen * lax.rsqrt(var + GN_EPS)
        g = g_ref[:, h * dv:(h + 1) * dv].astype(F32)
        o_ref[:, h * dv:(h + 1) * dv] = (g * jax.nn.sigmoid(g) * normed).astype(BF16)


def _retention(proj, d, batch, seq):
    t = proj.shape[0]
    heads = RET_HEADS
    ch = min(RET_KERNEL_CHUNK, seq)
    nc = seq // ch
    dqk = d // 2 // heads
    dv = d // heads
    log_gamma = jnp.log(1.0 - 2.0 ** (-5.0 - jnp.arange(heads, dtype=F32)))
    idx = jnp.arange(ch, dtype=F32)
    rel = idx[:, None] - idx[None, :]
    dec = jnp.where(rel[None] >= 0,
                    jnp.exp(jnp.maximum(rel, 0.0)[None] * log_gamma[:, None, None]), 0.0)
    xi = jnp.exp((idx + 1.0)[None, :] * log_gamma[:, None])[:, :, None]
    zeta = jnp.exp((ch - 1.0 - idx)[None, :] * log_gamma[:, None])[:, :, None]
    gam = jnp.exp(ch * log_gamma)[:, None, None]
    row = lambda b, c: b * nc + c
    return pl.pallas_call(
        functools.partial(_ret_kernel, heads=heads),
        out_shape=jax.ShapeDtypeStruct((t, d), BF16),
        grid=(batch, nc),
        in_specs=[
            pl.BlockSpec((ch, d // 2), lambda b, c: (row(b, c), 0)),
            pl.BlockSpec((ch, d // 2), lambda b, c: (row(b, c), 1)),
            pl.BlockSpec((ch, d), lambda b, c: (row(b, c), 1)),
            pl.BlockSpec((ch, d), lambda b, c: (row(b, c), 2)),
            pl.BlockSpec((heads, ch, ch), lambda b, c: (0, 0, 0)),
            pl.BlockSpec((heads, ch, 1), lambda b, c: (0, 0, 0)),
            pl.BlockSpec((heads, ch, 1), lambda b, c: (0, 0, 0)),
            pl.BlockSpec((heads, 1, 1), lambda b, c: (0, 0, 0)),
        ],
        out_specs=pl.BlockSpec((ch, d), lambda b, c: (row(b, c), 0)),
        scratch_shapes=[pltpu.VMEM((heads, dqk, dv), F32)],
        compiler_params=_params(("parallel", "arbitrary")),
        name="retention",
    )(proj, proj, proj, proj, dec, xi, zeta, gam)


def _mix_kernel(*refs, cn, ncast):
    og_ref, yg_ref, ga_ref, gb_ref, h_ref, wo_ref, wv_ref, wg_ref, wout_ref = refs[:9]
    cast_in = refs[9:9 + ncast]
    o_ref = refs[9 + ncast]
    cast_out = refs[10 + ncast:]
    _run_casts(cast_in, cast_out)
    d = h_ref.shape[1]
    og = og_ref[...]
    yg = jnp.concatenate([yg_ref[k] for k in range(yg_ref.shape[0])], axis=-1).astype(BF16)
    merged = []
    for j in range(d // cn):
        cols = slice(j * cn, (j + 1) * cn)
        y_b = _dot(og, wo_ref[:, cols])
        y_a = _dot(yg, wv_ref[:, cols]) * jax.nn.sigmoid(_dot(yg, wg_ref[:, cols]))
        merged.append((jax.nn.sigmoid(ga_ref[:, cols].astype(F32)) * y_a
                       + jax.nn.sigmoid(gb_ref[:, cols].astype(F32)) * y_b).astype(BF16))
    o_ref[...] = h_ref[...] + _dot(jnp.concatenate(merged, axis=-1), wout_ref[...])


def _mix(og, yg, proj, h, w_o, w_v, w_g, w_out, casts=(), *, tm=256, cn=512):
    t, d = h.shape
    nlb = yg.shape[0]
    sw = nlb * LANES
    tm, cn = min(tm, t), min(cn, d)
    cast_specs = [pl.BlockSpec(*_cast_block(c.shape, t // tm, 1)) for c in casts]
    return pl.pallas_call(
        functools.partial(_mix_kernel, cn=cn, ncast=len(casts)),
        out_shape=[jax.ShapeDtypeStruct((t, d), F32)]
        + [jax.ShapeDtypeStruct(c.shape, BF16) for c in casts],
        grid=(t // tm, 1),
        in_specs=[
            pl.BlockSpec((tm, d), lambda i, j: (i, 0)),
            pl.BlockSpec((nlb, tm, LANES), lambda i, j: (0, i, 0)),
            pl.BlockSpec((tm, d), lambda i, j: (i, 3)),
            pl.BlockSpec((tm, d), lambda i, j: (i, 4)),
            pl.BlockSpec((tm, d), lambda i, j: (i, 0)),
            _resident((d, d)), _resident((sw, d)), _resident((sw, d)), _resident((d, d)),
        ] + cast_specs,
        out_specs=[pl.BlockSpec((tm, d), lambda i, j: (i, 0))] + cast_specs,
        compiler_params=_params(("arbitrary", "arbitrary")),
        name="mix",
    )(og, yg, proj, proj, h, w_o, w_v, w_g, w_out, *casts)


def _xattn_kernel(h_ref, g_ref, wq_ref, k_ref, v_ref, wo_ref, o_ref, *, heads):
    h = h_ref[...]
    d = h.shape[1]
    dh = d // heads
    xn = _rms(h, g_ref[...]).astype(BF16)
    outs = []
    for hd in range(heads):
        cols = slice(hd * dh, (hd + 1) * dh)
        q = _dot(xn, wq_ref[:, cols])
        s = _dot_nt(q.astype(BF16), k_ref[0, :, cols]) * (dh ** -0.5)
        e = jnp.exp(s - jnp.max(s, axis=-1, keepdims=True))
        p = e / jnp.sum(e, axis=-1, keepdims=True)
        outs.append(_dot(p.astype(BF16), v_ref[0, :, cols]).astype(BF16))
    o_ref[...] = h + _dot(jnp.concatenate(outs, axis=-1), wo_ref[...])


def _xattn(h, gain, wq, kmem, vmem, wo, batch, *, tm=1024):
    t, d = h.shape
    seq = t // batch
    mlen = kmem.shape[1]
    tm = min(tm, seq)
    nt = seq // tm
    return pl.pallas_call(
        functools.partial(_xattn_kernel, heads=XATTN_HEADS),
        out_shape=jax.ShapeDtypeStruct((t, d), F32),
        grid=(batch, nt),
        in_specs=[
            pl.BlockSpec((tm, d), lambda b, i: (b * nt + i, 0)),
            _resident((1, d)),
            _resident((d, d)),
            pl.BlockSpec((1, mlen, d), lambda b, i: (b, 0, 0)),
            pl.BlockSpec((1, mlen, d), lambda b, i: (b, 0, 0)),
            _resident((d, d)),
        ],
        out_specs=pl.BlockSpec((tm, d), lambda b, i: (b * nt + i, 0)),
        compiler_params=_params(("arbitrary", "arbitrary")),
        name="xattn",
    )(h, gain.reshape(1, d), wq, kmem, vmem, wo)


def kernel(x, mem, ffn1_norm, ffn1_w1, ffn1_w3, ffn1_w2, mix_norm, w_in, s5_a_re, s5_a_im, s5_log_dt, s5_b_re, s5_b_im, s5_c_re, s5_c_im, s5_d, s5_glu_v, s5_glu_g, ret_w_o, w_out, xattn_norm, mem_norm, xattn_wq, xattn_wk, xattn_wv, xattn_wo, ffn2_norm, ffn2_w1, ffn2_w3, ffn2_w2, final_norm):
    batch, seq, d = x.shape
    mlen = mem.shape[1]
    depth = ffn1_w1.shape[0]
    t = batch * seq
    s5_width = s5_d.shape[1]
    bf = lambda w: w.astype(BF16)

    h = x.reshape(t, d)
    mem2 = mem.reshape(batch * mlen, d)
    for l in range(depth):
        last = l == depth - 1
        later = (w_in, s5_glu_v, s5_glu_g, ret_w_o, w_out, xattn_wq, xattn_wk, xattn_wv, xattn_wo)
        h, c_in, c_gv, c_gg, c_ro, c_out, c_wq, c_wk, c_wv, c_wo = _ffn(
            h, ffn1_norm[l], bf(ffn1_w1[l]), bf(ffn1_w3[l]), bf(ffn1_w2[l]),
            casts=[w[l] for w in later])
        proj, us5 = _inproj(h, mix_norm[l], c_in, s5_width, seq)
        toep, s_in, s_out, lam, dtile = _s5_weights(
            s5_a_re[l], s5_a_im[l], s5_log_dt[l], s5_b_re[l], s5_b_im[l],
            s5_c_re[l], s5_c_im[l], s5_d[l])
        yg = _s5(us5, toep, s_in, s_out, lam, dtile, batch)
        og = _retention(proj, d, batch, seq)
        h, c_f1, c_f3, c_f2 = _mix(og, yg, proj, h, c_ro, c_gv, c_gg, c_out,
                                   casts=[ffn2_w1[l], ffn2_w3[l], ffn2_w2[l]])
        kmem, vmem = _normmm(mem2, mem_norm[l], c_wk, c_wv)
        h = _xattn(h, xattn_norm[l], c_wq, kmem.reshape(batch, mlen, d),
                   vmem.reshape(batch, mlen, d), c_wo, batch)
        h, = _ffn(h, ffn2_norm[l], c_f1, c_f3, c_f2, final_norm if last else None)
    if depth == 0:
        raise ValueError("depth must be >= 1")
    return h.reshape(batch, seq, d)
```

```python
import functools

import jax
import jax.numpy as jnp
from jax import lax
from jax.experimental import pallas as pl
from jax.experimental.pallas import tpu as pltpu

F32 = jnp.float32
BF16 = jnp.bfloat16

RMS_EPS = 1e-6
GN_EPS = 1e-5
ROPE_BASE = 10000.0
S5_GROUP = 16
RET_HEADS = 4
XATTN_HEADS = 4
RET_KERNEL_CHUNK = 256
S5_CHUNK = 8
LANES = 128
GROUPS_PER_LANE_BLOCK = LANES // S5_GROUP
VMEM_LIMIT_BYTES = 60 * 1024 * 1024
MXU_TILE = 256


def _params(semantics):
    return pltpu.CompilerParams(dimension_semantics=semantics,
                                vmem_limit_bytes=VMEM_LIMIT_BYTES)


def _resident(shape):
    return pl.BlockSpec(shape, lambda *_: (0,) * len(shape), pipeline_mode=pl.Buffered(1))


def _rms(x, gain):
    ms = jnp.mean(x * x, axis=-1, keepdims=True)
    return x * lax.rsqrt(ms + RMS_EPS) * gain


def _dot(a, b):
    return jnp.dot(a, b, preferred_element_type=F32)


def _dot_nt(a, b):
    return lax.dot_general(a, b, (((1,), (1,)), ((), ())), preferred_element_type=F32)


def _dot_tn(a, b):
    return lax.dot_general(a, b, (((0,), (0,)), ((), ())), preferred_element_type=F32)


def _run_casts(cast_in, cast_out):
    for src_ref, dst_ref in zip(cast_in, cast_out):
        dst_ref[...] = src_ref[...].astype(BF16)


def _ffn_kernel(*refs, final, ncast):
    nin = 6 if final else 5
    x_ref, g_ref, w1_ref, w3_ref, w2_ref = refs[:5]
    fg_ref = refs[5] if final else None
    cast_in = refs[nin:nin + ncast]
    o_ref = refs[nin + ncast]
    cast_out = refs[nin + ncast + 1:nin + 2 * ncast + 1]
    xn_ref = refs[-1]
    j = pl.program_id(1)

    _run_casts(cast_in, cast_out)

    last = pl.num_programs(1) - 1

    def half_swiglu(xn):
        a = _dot(xn, w1_ref[...])
        b = _dot(xn, w3_ref[...])
        mid = (a * jax.nn.sigmoid(a) * b * 0.5).astype(BF16)
        return _dot(mid, w2_ref[...])

    @pl.when(j == 0)
    def _():
        x = x_ref[...]
        xn = _rms(x, g_ref[...]).astype(BF16)
        xn_ref[...] = xn
        o_ref[...] = x + half_swiglu(xn)

    @pl.when(jnp.logical_and(j > 0, j < last) if final else j > 0)
    def _():
        o_ref[...] += half_swiglu(xn_ref[...])

    if final:
        @pl.when(j == last)
        def _():
            o_ref[...] = _rms(o_ref[...] + half_swiglu(xn_ref[...]), fg_ref[...])


def _first_step_only(nt):
    return lambda i, j: (jnp.minimum(i + jnp.minimum(j, 1), nt - 1), 0)


def _cast_block(shape, nt, nf):
    r, c = shape

    def tiled(rows, cols):
        return rows % 16 == 0 and cols % LANES == 0

    if r % nt == 0 and c % nf == 0 and tiled(r // nt, c // nf):
        return (r // nt, c // nf), lambda i, j: (i, j)
    if r % nf == 0 and c % nt == 0 and tiled(r // nf, c // nt):
        return (r // nf, c // nt), lambda i, j: (j, i)
    nrow = max(n for n in range(1, nt + 1) if r % n == 0 and (r // n) % 16 == 0)
    ncol = max(n for n in range(1, nf + 1) if c % n == 0 and (c // n) % LANES == 0)
    return ((r // nrow, c // ncol),
            lambda i, j: (jnp.minimum(i, nrow - 1), jnp.minimum(j, ncol - 1)))


def _ffn(x, gain, w1, w3, w2, final_gain=None, casts=(), *, tm=1024, tf=512):
    t, d = x.shape
    f = w1.shape[1]
    tm, tf = min(tm, t), min(tf, f)
    final = final_gain is not None
    in_specs = [
        pl.BlockSpec((tm, d), _first_step_only(t // tm)),
        pl.BlockSpec((1, d), lambda i, j: (0, 0)),
        pl.BlockSpec((d, tf), lambda i, j: (0, j)),
        pl.BlockSpec((d, tf), lambda i, j: (0, j)),
        pl.BlockSpec((tf, d), lambda i, j: (j, 0)),
    ]
    args = [x, gain.reshape(1, d), w1, w3, w2]
    if final:
        in_specs.append(pl.BlockSpec((1, d), lambda i, j: (0, 0)))
        args.append(final_gain.reshape(1, d))
    cast_specs = [pl.BlockSpec(*_cast_block(w.shape, t // tm, f // tf)) for w in casts]
    outs = pl.pallas_call(
        functools.partial(_ffn_kernel, final=final, ncast=len(casts)),
        out_shape=[jax.ShapeDtypeStruct((t, d), F32)]
        + [jax.ShapeDtypeStruct(w.shape, BF16) for w in casts],
        grid=(t // tm, f // tf),
        in_specs=in_specs + cast_specs,
        out_specs=[pl.BlockSpec((tm, d), lambda i, j: (i, 0))] + cast_specs,
        scratch_shapes=[pltpu.VMEM((tm, d), BF16)],
        compiler_params=_params(("arbitrary", "arbitrary")),
        name="ffn_final" if final else "ffn",
    )(*args, *casts)
    return outs


def _rotary(t, cos, sin, heads):
    dh = t.shape[1] // heads
    half = dh // 2
    out = []
    for h in range(heads):
        t1, t2 = t[:, h * dh:h * dh + half], t[:, h * dh + half:(h + 1) * dh]
        out += [t1 * cos - t2 * sin, t1 * sin + t2 * cos]
    return jnp.concatenate(out, axis=-1)


def _inproj_kernel(x_ref, g_ref, wa_ref, wb_ref, cos_ref, sin_ref, proj_ref, us5_ref, xn_ref,
                   *, heads):
    j = pl.program_id(1)
    tn = wa_ref.shape[1]

    @pl.when(j == 0)
    def _():
        xn = _rms(x_ref[...], g_ref[...]).astype(BF16)
        xn_ref[...] = xn
        res = _dot(xn, wa_ref[...])
        for k in range(us5_ref.shape[0]):
            us5_ref[k] = res[:, k * LANES:(k + 1) * LANES]

    @pl.when(j == 1)
    def _():
        xn = xn_ref[...]
        cos, sin = cos_ref[...], sin_ref[...]
        q = _rotary(_dot(xn, wa_ref[...]), cos, sin, heads)
        k = _rotary(_dot(xn, wb_ref[...]), cos, sin, heads) * ((tn // heads) ** -0.5)
        proj_ref[:, :tn] = q.astype(BF16)
        proj_ref[:, tn:] = k.astype(BF16)

    @pl.when(j > 1)
    def _():
        xn = xn_ref[...]
        proj_ref[:, :tn] = _dot(xn, wa_ref[...]).astype(BF16)
        proj_ref[:, tn:] = _dot(xn, wb_ref[...]).astype(BF16)


def _rope_tables(seq, dh):
    inv = ROPE_BASE ** (-jnp.arange(0, dh, 2, dtype=F32) / dh)
    ang = jnp.arange(seq, dtype=F32)[:, None] * inv[None, :]
    return jnp.cos(ang), jnp.sin(ang)


def _inproj(x, gain, w, s5_width, seq, *, tm=1024):
    t, d = x.shape
    n = w.shape[1]
    tn = s5_width
    tm = min(tm, seq)
    assert seq % tm == 0 and tn == d // 2, (seq, tm, tn, d)
    pairs, odd = divmod(n // tn - 1, 2)
    assert odd == 0, (n, tn)
    nt, nf = t // tm, 1 + pairs
    half = tn // RET_HEADS // 2
    cos, sin = _rope_tables(seq, 2 * half)
    pos = pl.BlockSpec((tm, half), lambda i, j: (i % (seq // tm), 0))
    return pl.pallas_call(
        functools.partial(_inproj_kernel, heads=RET_HEADS),
        out_shape=[jax.ShapeDtypeStruct((t, n - tn), BF16),
                   jax.ShapeDtypeStruct((tn // LANES, t, LANES), F32)],
        grid=(nt, nf),
        in_specs=[
            pl.BlockSpec((tm, d), _first_step_only(nt)),
            pl.BlockSpec((1, d), lambda i, j: (0, 0)),
            pl.BlockSpec((d, tn), lambda i, j: (0, jnp.maximum(2 * j - 1, 0))),
            pl.BlockSpec((d, tn), lambda i, j: (0, 2 * jnp.maximum(j, 1))),
            pos, pos,
        ],
        out_specs=[
            pl.BlockSpec((tm, 2 * tn), lambda i, j: (i, jnp.maximum(j - 1, 0))),
            pl.BlockSpec((tn // LANES, tm, LANES), lambda i, j: (0, i, 0)),
        ],
        scratch_shapes=[pltpu.VMEM((tm, d), BF16)],
        compiler_params=_params(("arbitrary", "arbitrary")),
        name="inproj",
    )(x, gain.reshape(1, d), w, w, cos, sin)


def _normmm_kernel(x_ref, g_ref, wk_ref, wv_ref, k_ref, v_ref, xn_ref):
    @pl.when(pl.program_id(0) == 0)
    def _():
        xn_ref[...] = _rms(x_ref[...], g_ref[...]).astype(BF16)

    xn = xn_ref[...]
    k_ref[...] = _dot(xn, wk_ref[...]).astype(BF16)
    v_ref[...] = _dot(xn, wv_ref[...]).astype(BF16)


def _normmm(x, gain, wk, wv, *, tn=1024):
    t, d = x.shape
    n = wk.shape[1]
    tn = min(tn, n)
    col = lambda rows: pl.BlockSpec((rows, tn), lambda j: (0, j))
    return pl.pallas_call(
        _normmm_kernel,
        out_shape=[jax.ShapeDtypeStruct((t, n), BF16)] * 2,
        grid=(n // tn,),
        in_specs=[_resident((t, d)), _resident((1, d)), col(d), col(d)],
        out_specs=[col(t), col(t)],
        scratch_shapes=[pltpu.VMEM((t, d), BF16)],
        compiler_params=_params(("arbitrary",)),
        name="normmm",
    )(x, gain.reshape(1, d), wk, wv)


def _s5_kernel(u_ref, m_ref, win_ref, wout_ref, lam_ref, d_ref, y_ref, z_ref, s_ref, *, tc, nb):
    rb = z_ref.shape[0] // nb
    seq = rb * tc
    u32 = jnp.concatenate(
        [jnp.concatenate([u_ref[pl.ds(b * seq + j, rb, stride=tc), :] for j in range(tc)], axis=-1)
         for b in range(nb)], axis=0)
    u = u32.astype(BF16)
    z_ref[...] = _dot(u, win_ref[0])
    ns = z_ref.shape[1] // 2
    lam_re = lam_ref[0, :, :ns]
    lam_im = lam_ref[0, :, ns:]

    def step(c, carry):
        out = []
        for b in range(nb):
            s_re, s_im = carry[2 * b], carry[2 * b + 1]
            row = pl.ds(b * rb + c, 1)
            s_ref[row, :ns] = s_re
            s_ref[row, ns:] = s_im
            out += [lam_re * s_re - lam_im * s_im + z_ref[row, :ns],
                    lam_re * s_im + lam_im * s_re + z_ref[row, ns:]]
        return tuple(out)

    zero = jnp.zeros((1, ns), F32)
    lax.fori_loop(0, rb, step, (zero,) * (2 * nb), unroll=8)

    s = s_ref[...].astype(BF16)
    per_tile = MXU_TILE // LANES
    for cb in range(tc // per_tile):
        cols = slice(cb * MXU_TILE, (cb + 1) * MXU_TILE)
        kmax = (cb + 1) * MXU_TILE
        y = (_dot(u[:, :kmax], m_ref[0, :kmax, cols]) + _dot(s, wout_ref[0, :, cols])
             + d_ref[0, :, cols] * u32[:, cols])
        y = jax.nn.gelu(y)
        for b in range(nb):
            for i in range(per_tile):
                y_ref[pl.ds(b * seq + cb * per_tile + i, rb, stride=tc), :] = (
                    y[b * rb:(b + 1) * rb, i * LANES:(i + 1) * LANES])


def _s5(u, m, w_in, w_out, lam, dtile, batch, *, nb=1):
    nlb, t, _ = u.shape
    seq = t // batch
    tc = S5_CHUNK
    _, width, nstate = w_in.shape
    nb = nb if batch % nb == 0 else 1
    rows = nb * seq // tc
    return pl.pallas_call(
        functools.partial(_s5_kernel, tc=tc, nb=nb),
        out_shape=jax.ShapeDtypeStruct((nlb, t, LANES), F32),
        grid=(nlb, batch // nb),
        in_specs=[
            pl.BlockSpec((None, nb * seq, LANES), lambda k, b: (k, b, 0)),
            pl.BlockSpec((1, width, width), lambda k, b: (k, 0, 0)),
            pl.BlockSpec((1, width, nstate), lambda k, b: (k, 0, 0)),
            pl.BlockSpec((1, nstate, width), lambda k, b: (k, 0, 0)),
            pl.BlockSpec((1, 1, nstate), lambda k, b: (k, 0, 0)),
            pl.BlockSpec((1, 1, width), lambda k, b: (k, 0, 0)),
        ],
        out_specs=pl.BlockSpec((None, nb * seq, LANES), lambda k, b: (k, b, 0)),
        scratch_shapes=[pltpu.VMEM((rows, nstate), F32), pltpu.VMEM((rows, nstate), F32)],
        compiler_params=_params(("parallel", "arbitrary")),
        name="s5",
    )(u, m, w_in, w_out, lam, dtile)


def _log2(n):
    assert n & (n - 1) == 0, n
    return n.bit_length() - 1


def _s5prep_kernel(bre_ref, bim_ref, cre_ref, cim_ref, lrow_ref, lcol_ref,
                   toep_ref, win_ref, wout_ref, lam_ref, *, tc):
    ns, gi = bre_ref.shape[1], bre_ref.shape[2]
    lanes, p = cre_ref.shape[1], cre_ref.shape[2]

    def iota(shape, dim):
        return lax.broadcasted_iota(jnp.int32, shape, dim)

    def grp(x, size):
        return lax.shift_right_logical(x, _log2(size))

    def split(a):
        hi = a.astype(BF16)
        return hi, (a - hi.astype(F32)).astype(BF16)

    def dot3(a, b):
        return _dot(a[0], b[0]) + _dot(a[0], b[1]) + _dot(a[1], b[0])

    e_i = ((iota((gi, lanes), 1) & (gi - 1)) == iota((gi, lanes), 0)).astype(BF16)
    e_p = ((iota((p, ns), 1) & (p - 1)) == iota((p, ns), 0)).astype(BF16)

    def repeat(a, e):
        hi, lo = split(a)
        return _dot(hi, e) + _dot(lo, e)

    mask_b = grp(iota((ns, lanes), 0), p) == grp(iota((ns, lanes), 1), gi)
    mask_c = grp(iota((lanes, ns), 0), gi) == grp(iota((lanes, ns), 1), p)

    def bdiag_b(ref):
        return jnp.where(mask_b, repeat(ref[0], e_i), 0.0).T

    def bdiag_c(ref):
        return jnp.where(mask_c, repeat(ref[0], e_p), 0.0).T

    b_re, b_im = bdiag_b(bre_ref), bdiag_b(bim_ref)
    c_re, c_im = bdiag_c(cre_ref), bdiag_c(cim_ref)
    c_re2, c_im2 = split(c_re), split(c_im)
    l_re, l_im = lrow_ref[0, 0:1, :], lrow_ref[0, 1:2, :]
    lc_re, lc_im = lcol_ref[0, :, 0:1], lcol_ref[0, :, 1:2]
    p_re, p_im = jnp.ones_like(l_re), jnp.zeros_like(l_im)
    q_re, q_im = jnp.ones_like(lc_re), jnp.zeros_like(lc_im)
    zero_tile = jnp.zeros((lanes, lanes), BF16)

    def tile(i):
        return slice(i * lanes, (i + 1) * lanes)

    for n in range(tc):
        lb_re = b_re * p_re - b_im * p_im
        lb_im = b_re * p_im + b_im * p_re
        win_ref[0, tile(tc - 1 - n), :ns] = lb_re.astype(BF16)
        win_ref[0, tile(tc - 1 - n), ns:] = lb_im.astype(BF16)
        kern = (dot3(split(lb_re), c_re2) - dot3(split(lb_im), c_im2)).astype(BF16)
        for j in range(tc - n):
            toep_ref[0, tile(j), tile(j + n)] = kern
        if n:
            for t in range(tc - n):
                toep_ref[0, tile(t + n), tile(t)] = zero_tile
        p_re, p_im = p_re * l_re - p_im * l_im, p_re * l_im + p_im * l_re
        q_re, q_im = q_re * lc_re - q_im * lc_im, q_re * lc_im + q_im * lc_re
        wout_ref[0, :ns, tile(n)] = (c_re * q_re - c_im * q_im).astype(BF16)
        wout_ref[0, ns:, tile(n)] = (-(c_re * q_im + c_im * q_re)).astype(BF16)
    lam_ref[0, :, :ns] = p_re
    lam_ref[0, :, ns:] = p_im


def _s5_weights(a_re, a_im, log_dt, b_re, b_im, c_re, c_im, d_skip):
    tc = S5_CHUNK
    g, p = a_re.shape
    gl = GROUPS_PER_LANE_BLOCK
    nlb = g // gl
    ns = gl * p
    dt = jnp.exp(log_dt)[:, None]
    mag = jnp.exp(a_re * dt)
    l_re = mag * jnp.cos(a_im * dt)
    l_im = mag * jnp.sin(a_im * dt)
    den = a_re * a_re + a_im * a_im
    n_re = l_re - 1.0
    n_im = l_im
    f_re = (n_re * a_re + n_im * a_im) / den
    f_im = (n_im * a_re - n_re * a_im) / den
    bb_re = (f_re[..., None] * b_re - f_im[..., None] * b_im).reshape(nlb, ns, S5_GROUP)
    bb_im = (f_re[..., None] * b_im + f_im[..., None] * b_re).reshape(nlb, ns, S5_GROUP)
    lrow = jnp.stack([l_re.reshape(nlb, ns), l_im.reshape(nlb, ns)], axis=1)
    lcol = jnp.stack([l_re.reshape(nlb, ns), l_im.reshape(nlb, ns)], axis=2)
    width = tc * LANES
    blk = lambda *shape: pl.BlockSpec((1,) + shape, lambda k: (k, 0, 0))
    toep, w_in, w_out, lam = pl.pallas_call(
        functools.partial(_s5prep_kernel, tc=tc),
        out_shape=(jax.ShapeDtypeStruct((nlb, width, width), BF16),
                   jax.ShapeDtypeStruct((nlb, width, 2 * ns), BF16),
                   jax.ShapeDtypeStruct((nlb, 2 * ns, width), BF16),
                   jax.ShapeDtypeStruct((nlb, 1, 2 * ns), F32)),
        grid=(nlb,),
        in_specs=[blk(ns, S5_GROUP), blk(ns, S5_GROUP), blk(LANES, p), blk(LANES, p),
                  blk(2, ns), blk(ns, 2)],
        out_specs=(blk(width, width), blk(width, 2 * ns), blk(2 * ns, width),
                   blk(1, 2 * ns)),
        compiler_params=_params(("parallel",)),
        name="s5prep",
    )(bb_re, bb_im, c_re.reshape(nlb, LANES, p), c_im.reshape(nlb, LANES, p), lrow, lcol)
    dtile = jnp.tile(d_skip.reshape(nlb, 1, LANES), (1, 1, tc))
    return toep, w_in, w_out, lam, dtile


def _ret_kernel(q_ref, k_ref, v_ref, dec_ref, xi_ref, zeta_ref, gam_ref, o_ref, state_ref,
                *, heads):
    c = pl.program_id(1)

    @pl.when(c == 0)
    def _():
        state_ref[...] = jnp.zeros_like(state_ref)

    dqk = q_ref.shape[1] // heads
    dv = v_ref.shape[1] // heads
    for h in range(heads):
        qb = q_ref[:, h * dqk:(h + 1) * dqk]
        kb = k_ref[:, h * dqk:(h + 1) * dqk]
        v = v_ref[:, h * dv:(h + 1) * dv]
        scores = _dot_nt(qb, kb) * dec_ref[h]
        st = state_ref[h]
        out = _dot(scores.astype(BF16), v) + _dot(qb, st.astype(BF16)) * xi_ref[h]
        kz = (kb.astype(F32) * zeta_ref[h]).astype(BF16)
        state_ref[h] = st * gam_ref[h] + _dot_tn(kz, v)
        o_ref[:, h * dv:(h + 1) * dv] = out.astype(BF16)


def _retention(proj, d, batch, seq):
    t = proj.shape[0]
    heads = RET_HEADS
    ch = min(RET_KERNEL_CHUNK, seq)
    nc = seq // ch
    dqk = d // 2 // heads
    dv = d // heads
    log_gamma = jnp.log(1.0 - 2.0 ** (-5.0 - jnp.arange(heads, dtype=F32)))
    idx = jnp.arange(ch, dtype=F32)
    rel = idx[:, None] - idx[None, :]
    dec = jnp.where(rel[None] >= 0,
                    jnp.exp(jnp.maximum(rel, 0.0)[None] * log_gamma[:, None, None]), 0.0)
    xi = jnp.exp((idx + 1.0)[None, :] * log_gamma[:, None])[:, :, None]
    zeta = jnp.exp((ch - 1.0 - idx)[None, :] * log_gamma[:, None])[:, :, None]
    gam = jnp.exp(ch * log_gamma)[:, None, None]
    row = lambda b, c: b * nc + c
    return pl.pallas_call(
        functools.partial(_ret_kernel, heads=heads),
        out_shape=jax.ShapeDtypeStruct((t, d), BF16),
        grid=(batch, nc),
        in_specs=[
            pl.BlockSpec((ch, d // 2), lambda b, c: (row(b, c), 0)),
            pl.BlockSpec((ch, d // 2), lambda b, c: (row(b, c), 1)),
            pl.BlockSpec((ch, d), lambda b, c: (row(b, c), 1)),
            pl.BlockSpec((heads, ch, ch), lambda b, c: (0, 0, 0)),
            pl.BlockSpec((heads, ch, 1), lambda b, c: (0, 0, 0)),
            pl.BlockSpec((heads, ch, 1), lambda b, c: (0, 0, 0)),
            pl.BlockSpec((heads, 1, 1), lambda b, c: (0, 0, 0)),
        ],
        out_specs=pl.BlockSpec((ch, d), lambda b, c: (row(b, c), 0)),
        scratch_shapes=[pltpu.VMEM((heads, dqk, dv), F32)],
        compiler_params=_params(("parallel", "arbitrary")),
        name="retention",
    )(proj, proj, proj, dec, xi, zeta, gam)


def _mix_kernel(*refs, cn, ncast, heads):
    (ret_ref, g_ref, yg_ref, ga_ref, gb_ref, h_ref, wo_ref, wv_ref, wg_ref,
     wout_ref) = refs[:10]
    cast_in = refs[10:10 + ncast]
    o_ref = refs[10 + ncast]
    cast_out = refs[11 + ncast:]
    _run_casts(cast_in, cast_out)
    d = h_ref.shape[1]
    dv = d // heads
    blocks = [slice(j * cn, (j + 1) * cn) for j in range(d // cn)]
    yg = jnp.concatenate([yg_ref[k] for k in range(yg_ref.shape[0])], axis=-1).astype(BF16)
    y_a = [jax.nn.sigmoid(ga_ref[:, c].astype(F32)) * _dot(yg, wv_ref[:, c])
           * jax.nn.sigmoid(_dot(yg, wg_ref[:, c])) for c in blocks]
    og = []
    for hd in range(heads):
        cols = slice(hd * dv, (hd + 1) * dv)
        o = ret_ref[:, cols].astype(F32)
        cen = o - jnp.mean(o, axis=-1, keepdims=True)
        var = jnp.mean(cen * cen, axis=-1, keepdims=True)
        g = g_ref[:, cols].astype(F32)
        og.append((g * jax.nn.sigmoid(g) * (cen * lax.rsqrt(var + GN_EPS))).astype(BF16))
    og = jnp.concatenate(og, axis=-1)
    merged = [(y_a[j] + jax.nn.sigmoid(gb_ref[:, c].astype(F32)) * _dot(og, wo_ref[:, c]))
              .astype(BF16) for j, c in enumerate(blocks)]
    o_ref[...] = h_ref[...] + _dot(jnp.concatenate(merged, axis=-1), wout_ref[...])


def _mix(ret, yg, proj, h, w_o, w_v, w_g, w_out, casts=(), *, tm=256, cn=256):
    t, d = h.shape
    nlb = yg.shape[0]
    sw = nlb * LANES
    tm, cn = min(tm, t), min(cn, d)
    cast_specs = [pl.BlockSpec(*_cast_block(c.shape, t // tm, 1)) for c in casts]
    return pl.pallas_call(
        functools.partial(_mix_kernel, cn=cn, ncast=len(casts), heads=RET_HEADS),
        out_shape=[jax.ShapeDtypeStruct((t, d), F32)]
        + [jax.ShapeDtypeStruct(c.shape, BF16) for c in casts],
        grid=(t // tm, 1),
        in_specs=[
            pl.BlockSpec((tm, d), lambda i, j: (i, 0)),
            pl.BlockSpec((tm, d), lambda i, j: (i, 2)),
            pl.BlockSpec((nlb, tm, LANES), lambda i, j: (0, i, 0)),
            pl.BlockSpec((tm, d), lambda i, j: (i, 3)),
            pl.BlockSpec((tm, d), lambda i, j: (i, 4)),
            pl.BlockSpec((tm, d), lambda i, j: (i, 0)),
            _resident((d, d)), _resident((sw, d)), _resident((sw, d)), _resident((d, d)),
        ] + cast_specs,
        out_specs=[pl.BlockSpec((tm, d), lambda i, j: (i, 0))] + cast_specs,
        compiler_params=_params(("arbitrary", "arbitrary")),
        name="mix",
    )(ret, proj, yg, proj, proj, h, w_o, w_v, w_g, w_out, *casts)


def _xattn_kernel(h_ref, g_ref, wq_ref, k_ref, v_ref, wo_ref, o_ref, *, heads):
    h = h_ref[...]
    d = h.shape[1]
    dh = d // heads
    xn = _rms(h, g_ref[...]).astype(BF16)
    outs = []
    for hd in range(heads):
        cols = slice(hd * dh, (hd + 1) * dh)
        q = _dot(xn, wq_ref[:, cols])
        s = _dot_nt(q.astype(BF16), k_ref[0, :, cols]) * (dh ** -0.5)
        e = jnp.exp(s - jnp.max(s, axis=-1, keepdims=True))
        p = e / jnp.sum(e, axis=-1, keepdims=True)
        outs.append(_dot(p.astype(BF16), v_ref[0, :, cols]).astype(BF16))
    o_ref[...] = h + _dot(jnp.concatenate(outs, axis=-1), wo_ref[...])


def _xattn(h, gain, wq, kmem, vmem, wo, batch, *, tm=1024):
    t, d = h.shape
    seq = t // batch
    mlen = kmem.shape[1]
    tm = min(tm, seq)
    nt = seq // tm
    return pl.pallas_call(
        functools.partial(_xattn_kernel, heads=XATTN_HEADS),
        out_shape=jax.ShapeDtypeStruct((t, d), F32),
        grid=(batch, nt),
        in_specs=[
            pl.BlockSpec((tm, d), lambda b, i: (b * nt + i, 0)),
            _resident((1, d)),
            _resident((d, d)),
            pl.BlockSpec((1, mlen, d), lambda b, i: (b, 0, 0)),
            pl.BlockSpec((1, mlen, d), lambda b, i: (b, 0, 0)),
            _resident((d, d)),
        ],
        out_specs=pl.BlockSpec((tm, d), lambda b, i: (b * nt + i, 0)),
        compiler_params=_params(("arbitrary", "arbitrary")),
        name="xattn",
    )(h, gain.reshape(1, d), wq, kmem, vmem, wo)


def kernel(x, mem, ffn1_norm, ffn1_w1, ffn1_w3, ffn1_w2, mix_norm, w_in, s5_a_re, s5_a_im, s5_log_dt, s5_b_re, s5_b_im, s5_c_re, s5_c_im, s5_d, s5_glu_v, s5_glu_g, ret_w_o, w_out, xattn_norm, mem_norm, xattn_wq, xattn_wk, xattn_wv, xattn_wo, ffn2_norm, ffn2_w1, ffn2_w3, ffn2_w2, final_norm):
    batch, seq, d = x.shape
    mlen = mem.shape[1]
    depth = ffn1_w1.shape[0]
    t = batch * seq
    s5_width = s5_d.shape[1]
    bf = lambda w: w.astype(BF16)

    h = x.reshape(t, d)
    mem2 = mem.reshape(batch * mlen, d)
    for l in range(depth):
        last = l == depth - 1
        later = (w_in, s5_glu_v, s5_glu_g, ret_w_o, w_out, xattn_wq, xattn_wk, xattn_wv, xattn_wo)
        h, c_in, c_gv, c_gg, c_ro, c_out, c_wq, c_wk, c_wv, c_wo = _ffn(
            h, ffn1_norm[l], bf(ffn1_w1[l]), bf(ffn1_w3[l]), bf(ffn1_w2[l]),
            casts=[w[l] for w in later])
        proj, us5 = _inproj(h, mix_norm[l], c_in, s5_width, seq)
        toep, s_in, s_out, lam, dtile = _s5_weights(
            s5_a_re[l], s5_a_im[l], s5_log_dt[l], s5_b_re[l], s5_b_im[l],
            s5_c_re[l], s5_c_im[l], s5_d[l])
        yg = _s5(us5, toep, s_in, s_out, lam, dtile, batch)
        ret = _retention(proj, d, batch, seq)
        h, c_f1, c_f3, c_f2 = _mix(ret, yg, proj, h, c_ro, c_gv, c_gg, c_out,
                                   casts=[ffn2_w1[l], ffn2_w3[l], ffn2_w2[l]])
        kmem, vmem = _normmm(mem2, mem_norm[l], c_wk, c_wv)
        h = _xattn(h, xattn_norm[l], c_wq, kmem.reshape(batch, mlen, d),
                   vmem.reshape(batch, mlen, d), c_wo, batch)
        h, = _ffn(h, ffn2_norm[l], c_f1, c_f3, c_f2, final_norm if last else None)
    if depth == 0:
        raise ValueError("depth must be >= 1")
    return h.reshape(batch, seq, d)
```

```python
import functools

import jax
import jax.numpy as jnp
from jax import lax
from jax.experimental import pallas as pl
from jax.experimental.pallas import tpu as pltpu

F32 = jnp.float32
BF16 = jnp.bfloat16

RMS_EPS = 1e-6
GN_EPS = 1e-5
ROPE_BASE = 10000.0
S5_GROUP = 16
RET_HEADS = 4
XATTN_HEADS = 4
RET_KERNEL_CHUNK = 256
S5_CHUNK = 8
LANES = 128
GROUPS_PER_LANE_BLOCK = LANES // S5_GROUP
VMEM_LIMIT_BYTES = 60 * 1024 * 1024
MXU_TILE = 256
SCAN_STRIDE = 2


def _params(semantics):
    return pltpu.CompilerParams(dimension_semantics=semantics,
                                vmem_limit_bytes=VMEM_LIMIT_BYTES)


def _resident(shape):
    return pl.BlockSpec(shape, lambda *_: (0,) * len(shape), pipeline_mode=pl.Buffered(1))


def _rms(x, gain):
    ms = jnp.mean(x * x, axis=-1, keepdims=True)
    return x * lax.rsqrt(ms + RMS_EPS) * gain


def _dot(a, b):
    return jnp.dot(a, b, preferred_element_type=F32)


def _dot_nt(a, b):
    return lax.dot_general(a, b, (((1,), (1,)), ((), ())), preferred_element_type=F32)


def _dot_tn(a, b):
    return lax.dot_general(a, b, (((0,), (0,)), ((), ())), preferred_element_type=F32)


def _run_casts(cast_in, cast_out):
    for src_ref, dst_ref in zip(cast_in, cast_out):
        dst_ref[...] = src_ref[...].astype(BF16)


def _ffn_kernel(*refs, final, ncast):
    nin = 6 if final else 5
    x_ref, g_ref, w1_ref, w3_ref, w2_ref = refs[:5]
    fg_ref = refs[5] if final else None
    cast_in = refs[nin:nin + ncast]
    o_ref = refs[nin + ncast]
    cast_out = refs[nin + ncast + 1:nin + 2 * ncast + 1]
    xn_ref = refs[-1]
    j = pl.program_id(1)

    _run_casts(cast_in, cast_out)

    last = pl.num_programs(1) - 1

    def half_swiglu(xn):
        a = _dot(xn, w1_ref[...])
        b = _dot(xn, w3_ref[...])
        mid = (a * jax.nn.sigmoid(a) * b * 0.5).astype(BF16)
        return _dot(mid, w2_ref[...])

    @pl.when(j == 0)
    def _():
        x = x_ref[...]
        xn = _rms(x, g_ref[...]).astype(BF16)
        xn_ref[...] = xn
        o_ref[...] = x + half_swiglu(xn)

    @pl.when(jnp.logical_and(j > 0, j < last) if final else j > 0)
    def _():
        o_ref[...] += half_swiglu(xn_ref[...])

    if final:
        @pl.when(j == last)
        def _():
            o_ref[...] = _rms(o_ref[...] + half_swiglu(xn_ref[...]), fg_ref[...])


def _first_step_only(nt):
    return lambda i, j: (jnp.minimum(i + jnp.minimum(j, 1), nt - 1), 0)


def _cast_block(shape, nt, nf):
    r, c = shape

    def tiled(rows, cols):
        return rows % 16 == 0 and cols % LANES == 0

    if r % nt == 0 and c % nf == 0 and tiled(r // nt, c // nf):
        return (r // nt, c // nf), lambda i, j: (i, j)
    if r % nf == 0 and c % nt == 0 and tiled(r // nf, c // nt):
        return (r // nf, c // nt), lambda i, j: (j, i)
    nrow = max(n for n in range(1, nt + 1) if r % n == 0 and (r // n) % 16 == 0)
    ncol = max(n for n in range(1, nf + 1) if c % n == 0 and (c // n) % LANES == 0)
    return ((r // nrow, c // ncol),
            lambda i, j: (jnp.minimum(i, nrow - 1), jnp.minimum(j, ncol - 1)))


def _ffn(x, gain, w1, w3, w2, final_gain=None, casts=(), *, tm=1024, tf=512):
    t, d = x.shape
    f = w1.shape[1]
    tm, tf = min(tm, t), min(tf, f)
    final = final_gain is not None
    in_specs = [
        pl.BlockSpec((tm, d), _first_step_only(t // tm)),
        pl.BlockSpec((1, d), lambda i, j: (0, 0)),
        pl.BlockSpec((d, tf), lambda i, j: (0, j)),
        pl.BlockSpec((d, tf), lambda i, j: (0, j)),
        pl.BlockSpec((tf, d), lambda i, j: (j, 0)),
    ]
    args = [x, gain.reshape(1, d), w1, w3, w2]
    if final:
        in_specs.append(pl.BlockSpec((1, d), lambda i, j: (0, 0)))
        args.append(final_gain.reshape(1, d))
    cast_specs = [pl.BlockSpec(*_cast_block(w.shape, t // tm, f // tf)) for w in casts]
    outs = pl.pallas_call(
        functools.partial(_ffn_kernel, final=final, ncast=len(casts)),
        out_shape=[jax.ShapeDtypeStruct((t, d), F32)]
        + [jax.ShapeDtypeStruct(w.shape, BF16) for w in casts],
        grid=(t // tm, f // tf),
        in_specs=in_specs + cast_specs,
        out_specs=[pl.BlockSpec((tm, d), lambda i, j: (i, 0))] + cast_specs,
        scratch_shapes=[pltpu.VMEM((tm, d), BF16)],
        compiler_params=_params(("arbitrary", "arbitrary")),
        name="ffn_final" if final else "ffn",
    )(*args, *casts)
    return outs


def _rotary(t, cos, sin, heads):
    dh = t.shape[1] // heads
    half = dh // 2
    out = []
    for h in range(heads):
        t1, t2 = t[:, h * dh:h * dh + half], t[:, h * dh + half:(h + 1) * dh]
        out += [t1 * cos - t2 * sin, t1 * sin + t2 * cos]
    return jnp.concatenate(out, axis=-1)


def _inproj_kernel(x_ref, g_ref, wa_ref, wb_ref, cos_ref, sin_ref, proj_ref, us5_ref, xn_ref,
                   *, heads):
    j = pl.program_id(1)
    tn = wa_ref.shape[1]

    @pl.when(j == 0)
    def _():
        xn = _rms(x_ref[...], g_ref[...]).astype(BF16)
        xn_ref[...] = xn
        res = _dot(xn, wa_ref[...])
        for k in range(us5_ref.shape[0]):
            us5_ref[k] = res[:, k * LANES:(k + 1) * LANES]

    @pl.when(j == 1)
    def _():
        xn = xn_ref[...]
        cos, sin = cos_ref[...], sin_ref[...]
        q = _rotary(_dot(xn, wa_ref[...]), cos, sin, heads)
        k = _rotary(_dot(xn, wb_ref[...]), cos, sin, heads) * ((tn // heads) ** -0.5)
        proj_ref[:, :tn] = q.astype(BF16)
        proj_ref[:, tn:] = k.astype(BF16)

    @pl.when(j > 1)
    def _():
        xn = xn_ref[...]
        proj_ref[:, :tn] = _dot(xn, wa_ref[...]).astype(BF16)
        proj_ref[:, tn:] = _dot(xn, wb_ref[...]).astype(BF16)


def _rope_tables(seq, dh):
    inv = ROPE_BASE ** (-jnp.arange(0, dh, 2, dtype=F32) / dh)
    ang = jnp.arange(seq, dtype=F32)[:, None] * inv[None, :]
    return jnp.cos(ang), jnp.sin(ang)


def _inproj(x, gain, w, s5_width, seq, *, tm=1024):
    t, d = x.shape
    n = w.shape[1]
    tn = s5_width
    tm = min(tm, seq)
    assert seq % tm == 0 and tn == d // 2, (seq, tm, tn, d)
    pairs, odd = divmod(n // tn - 1, 2)
    assert odd == 0, (n, tn)
    nt, nf = t // tm, 1 + pairs
    half = tn // RET_HEADS // 2
    cos, sin = _rope_tables(seq, 2 * half)
    pos = pl.BlockSpec((tm, half), lambda i, j: (i % (seq // tm), 0))
    return pl.pallas_call(
        functools.partial(_inproj_kernel, heads=RET_HEADS),
        out_shape=[jax.ShapeDtypeStruct((t, n - tn), BF16),
                   jax.ShapeDtypeStruct((tn // LANES, t, LANES), F32)],
        grid=(nt, nf),
        in_specs=[
            pl.BlockSpec((tm, d), _first_step_only(nt)),
            pl.BlockSpec((1, d), lambda i, j: (0, 0)),
            pl.BlockSpec((d, tn), lambda i, j: (0, jnp.maximum(2 * j - 1, 0))),
            pl.BlockSpec((d, tn), lambda i, j: (0, 2 * jnp.maximum(j, 1))),
            pos, pos,
        ],
        out_specs=[
            pl.BlockSpec((tm, 2 * tn), lambda i, j: (i, jnp.maximum(j - 1, 0))),
            pl.BlockSpec((tn // LANES, tm, LANES), lambda i, j: (0, i, 0)),
        ],
        scratch_shapes=[pltpu.VMEM((tm, d), BF16)],
        compiler_params=_params(("arbitrary", "arbitrary")),
        name="inproj",
    )(x, gain.reshape(1, d), w, w, cos, sin)


def _normmm_kernel(x_ref, g_ref, wk_ref, wv_ref, k_ref, v_ref, xn_ref):
    @pl.when(pl.program_id(0) == 0)
    def _():
        xn_ref[...] = _rms(x_ref[...], g_ref[...]).astype(BF16)

    xn = xn_ref[...]
    k_ref[...] = _dot(xn, wk_ref[...]).astype(BF16)
    v_ref[...] = _dot(xn, wv_ref[...]).astype(BF16)


def _normmm(x, gain, wk, wv, *, tn=1024):
    t, d = x.shape
    n = wk.shape[1]
    tn = min(tn, n)
    col = lambda rows: pl.BlockSpec((rows, tn), lambda j: (0, j))
    return pl.pallas_call(
        _normmm_kernel,
        out_shape=[jax.ShapeDtypeStruct((t, n), BF16)] * 2,
        grid=(n // tn,),
        in_specs=[_resident((t, d)), _resident((1, d)), col(d), col(d)],
        out_specs=[col(t), col(t)],
        scratch_shapes=[pltpu.VMEM((t, d), BF16)],
        compiler_params=_params(("arbitrary",)),
        name="normmm",
    )(x, gain.reshape(1, d), wk, wv)


def _s5_kernel(u_ref, m_ref, win_ref, wout_ref, lam_ref, d_ref, y_ref, z_ref, s_ref, *, tc, nb):
    rb = z_ref.shape[0] // nb
    seq = rb * tc
    u32 = jnp.concatenate(
        [jnp.concatenate([u_ref[pl.ds(b * seq + j, rb, stride=tc), :] for j in range(tc)], axis=-1)
         for b in range(nb)], axis=0)
    u = u32.astype(BF16)
    z_ref[...] = _dot(u, win_ref[0])
    ns = z_ref.shape[1] // 2
    lam_re = lam_ref[0, :, :ns]
    lam_im = lam_ref[0, :, ns:]

    per_tile = MXU_TILE // LANES
    blocks = [slice(cb * MXU_TILE, (cb + 1) * MXU_TILE) for cb in range(tc // per_tile)]
    y_intra = [_dot(u[:, :(cb + 1) * MXU_TILE], m_ref[0, :(cb + 1) * MXU_TILE, cols])
               + d_ref[0, :, cols] * u32[:, cols] for cb, cols in enumerate(blocks)]

    def cmul_add(a_re, a_im, x_re, x_im, y_re, y_im):
        return a_re * x_re - a_im * x_im + y_re, a_re * x_im + a_im * x_re + y_im

    stride = SCAN_STRIDE
    assert rb % stride == 0, (rb, stride)
    zero = jnp.zeros((1, ns), F32)
    lamk = (lam_re, lam_im)
    for _ in range(stride - 1):
        lamk = cmul_add(lam_re, lam_im, *lamk, zero, zero)
    for b in range(nb):
        s = (zero, zero)
        for c in range(0, rb, stride):
            rows = [slice(b * rb + c + i, b * rb + c + i + 1) for i in range(stride)]
            z = [(z_ref[r, :ns], z_ref[r, ns:]) for r in rows]
            t, w = s, z[0]
            for i, r in enumerate(rows):
                s_ref[r, :ns], s_ref[r, ns:] = t
                if i + 1 < stride:
                    t = cmul_add(lam_re, lam_im, *t, *z[i])
                    w = cmul_add(lam_re, lam_im, *w, *z[i + 1])
            s = cmul_add(*lamk, *s, *w)

    s = s_ref[...].astype(BF16)
    for cb, cols in enumerate(blocks):
        y = jax.nn.gelu(y_intra[cb] + _dot(s, wout_ref[0, :, cols]))
        for b in range(nb):
            for i in range(per_tile):
                y_ref[pl.ds(b * seq + cb * per_tile + i, rb, stride=tc), :] = (
                    y[b * rb:(b + 1) * rb, i * LANES:(i + 1) * LANES])


def _s5(u, m, w_in, w_out, lam, dtile, batch, *, nb=1):
    nlb, t, _ = u.shape
    seq = t // batch
    tc = S5_CHUNK
    _, width, nstate = w_in.shape
    nb = nb if batch % nb == 0 else 1
    rows = nb * seq // tc
    return pl.pallas_call(
        functools.partial(_s5_kernel, tc=tc, nb=nb),
        out_shape=jax.ShapeDtypeStruct((nlb, t, LANES), F32),
        grid=(nlb, batch // nb),
        in_specs=[
            pl.BlockSpec((None, nb * seq, LANES), lambda k, b: (k, b, 0)),
            pl.BlockSpec((1, width, width), lambda k, b: (k, 0, 0)),
            pl.BlockSpec((1, width, nstate), lambda k, b: (k, 0, 0)),
            pl.BlockSpec((1, nstate, width), lambda k, b: (k, 0, 0)),
            pl.BlockSpec((1, 1, nstate), lambda k, b: (k, 0, 0)),
            pl.BlockSpec((1, 1, width), lambda k, b: (k, 0, 0)),
        ],
        out_specs=pl.BlockSpec((None, nb * seq, LANES), lambda k, b: (k, b, 0)),
        scratch_shapes=[pltpu.VMEM((rows, nstate), F32), pltpu.VMEM((rows, nstate), F32)],
        compiler_params=_params(("parallel", "arbitrary")),
        name="s5",
    )(u, m, w_in, w_out, lam, dtile)


def _log2(n):
    assert n & (n - 1) == 0, n
    return n.bit_length() - 1


def _s5prep_kernel(bre_ref, bim_ref, cre_ref, cim_ref, lrow_ref, lcol_ref,
                   toep_ref, win_ref, wout_ref, lam_ref, *, tc):
    ns, gi = bre_ref.shape[1], bre_ref.shape[2]
    lanes, p = cre_ref.shape[1], cre_ref.shape[2]

    def iota(shape, dim):
        return lax.broadcasted_iota(jnp.int32, shape, dim)

    def grp(x, size):
        return lax.shift_right_logical(x, _log2(size))

    def split(a):
        hi = a.astype(BF16)
        return hi, (a - hi.astype(F32)).astype(BF16)

    def dot3(a, b):
        return _dot(a[0], b[0]) + _dot(a[0], b[1]) + _dot(a[1], b[0])

    e_i = ((iota((gi, lanes), 1) & (gi - 1)) == iota((gi, lanes), 0)).astype(BF16)
    e_p = ((iota((p, ns), 1) & (p - 1)) == iota((p, ns), 0)).astype(BF16)

    def repeat(a, e):
        hi, lo = split(a)
        return _dot(hi, e) + _dot(lo, e)

    mask_b = grp(iota((ns, lanes), 0), p) == grp(iota((ns, lanes), 1), gi)
    mask_c = grp(iota((lanes, ns), 0), gi) == grp(iota((lanes, ns), 1), p)

    def bdiag_b(ref):
        return jnp.where(mask_b, repeat(ref[0], e_i), 0.0).T

    def bdiag_c(ref):
        return jnp.where(mask_c, repeat(ref[0], e_p), 0.0).T

    b_re, b_im = bdiag_b(bre_ref), bdiag_b(bim_ref)
    c_re, c_im = bdiag_c(cre_ref), bdiag_c(cim_ref)
    c_re2, c_im2 = split(c_re), split(c_im)
    l_re, l_im = lrow_ref[0, 0:1, :], lrow_ref[0, 1:2, :]
    lc_re, lc_im = lcol_ref[0, :, 0:1], lcol_ref[0, :, 1:2]
    p_re, p_im = jnp.ones_like(l_re), jnp.zeros_like(l_im)
    q_re, q_im = jnp.ones_like(lc_re), jnp.zeros_like(lc_im)
    zero_tile = jnp.zeros((lanes, lanes), BF16)

    def tile(i):
        return slice(i * lanes, (i + 1) * lanes)

    for n in range(tc):
        lb_re = b_re * p_re - b_im * p_im
        lb_im = b_re * p_im + b_im * p_re
        win_ref[0, tile(tc - 1 - n), :ns] = lb_re.astype(BF16)
        win_ref[0, tile(tc - 1 - n), ns:] = lb_im.astype(BF16)
        kern = (dot3(split(lb_re), c_re2) - dot3(split(lb_im), c_im2)).astype(BF16)
        for j in range(tc - n):
            toep_ref[0, tile(j), tile(j + n)] = kern
        if n:
            for t in range(tc - n):
                toep_ref[0, tile(t + n), tile(t)] = zero_tile
        p_re, p_im = p_re * l_re - p_im * l_im, p_re * l_im + p_im * l_re
        q_re, q_im = q_re * lc_re - q_im * lc_im, q_re * lc_im + q_im * lc_re
        wout_ref[0, :ns, tile(n)] = (c_re * q_re - c_im * q_im).astype(BF16)
        wout_ref[0, ns:, tile(n)] = (-(c_re * q_im + c_im * q_re)).astype(BF16)
    lam_ref[0, :, :ns] = p_re
    lam_ref[0, :, ns:] = p_im


def _s5_weights(a_re, a_im, log_dt, b_re, b_im, c_re, c_im, d_skip):
    tc = S5_CHUNK
    g, p = a_re.shape
    gl = GROUPS_PER_LANE_BLOCK
    nlb = g // gl
    ns = gl * p
    dt = jnp.exp(log_dt)[:, None]
    mag = jnp.exp(a_re * dt)
    l_re = mag * jnp.cos(a_im * dt)
    l_im = mag * jnp.sin(a_im * dt)
    den = a_re * a_re + a_im * a_im
    n_re = l_re - 1.0
    n_im = l_im
    f_re = (n_re * a_re + n_im * a_im) / den
    f_im = (n_im * a_re - n_re * a_im) / den
    bb_re = (f_re[..., None] * b_re - f_im[..., None] * b_im).reshape(nlb, ns, S5_GROUP)
    bb_im = (f_re[..., None] * b_im + f_im[..., None] * b_re).reshape(nlb, ns, S5_GROUP)
    lrow = jnp.stack([l_re.reshape(nlb, ns), l_im.reshape(nlb, ns)], axis=1)
    lcol = jnp.stack([l_re.reshape(nlb, ns), l_im.reshape(nlb, ns)], axis=2)
    width = tc * LANES
    blk = lambda *shape: pl.BlockSpec((1,) + shape, lambda k: (k, 0, 0))
    toep, w_in, w_out, lam = pl.pallas_call(
        functools.partial(_s5prep_kernel, tc=tc),
        out_shape=(jax.ShapeDtypeStruct((nlb, width, width), BF16),
                   jax.ShapeDtypeStruct((nlb, width, 2 * ns), BF16),
                   jax.ShapeDtypeStruct((nlb, 2 * ns, width), BF16),
                   jax.ShapeDtypeStruct((nlb, 1, 2 * ns), F32)),
        grid=(nlb,),
        in_specs=[blk(ns, S5_GROUP), blk(ns, S5_GROUP), blk(LANES, p), blk(LANES, p),
                  blk(2, ns), blk(ns, 2)],
        out_specs=(blk(width, width), blk(width, 2 * ns), blk(2 * ns, width),
                   blk(1, 2 * ns)),
        compiler_params=_params(("parallel",)),
        name="s5prep",
    )(bb_re, bb_im, c_re.reshape(nlb, LANES, p), c_im.reshape(nlb, LANES, p), lrow, lcol)
    dtile = jnp.tile(d_skip.reshape(nlb, 1, LANES), (1, 1, tc))
    return toep, w_in, w_out, lam, dtile


def _ret_kernel(q_ref, k_ref, v_ref, dec_ref, xi_ref, zeta_ref, gam_ref, o_ref, state_ref,
                *, heads):
    c = pl.program_id(1)

    @pl.when(c == 0)
    def _():
        state_ref[...] = jnp.zeros_like(state_ref)

    dqk = q_ref.shape[1] // heads
    dv = v_ref.shape[1] // heads
    for h in range(heads):
        qb = q_ref[:, h * dqk:(h + 1) * dqk]
        kb = k_ref[:, h * dqk:(h + 1) * dqk]
        v = v_ref[:, h * dv:(h + 1) * dv]
        scores = _dot_nt(qb, kb) * dec_ref[h]
        st = state_ref[h]
        out = _dot(scores.astype(BF16), v) + _dot(qb, st.astype(BF16)) * xi_ref[h]
        kz = (kb.astype(F32) * zeta_ref[h]).astype(BF16)
        state_ref[h] = st * gam_ref[h] + _dot_tn(kz, v)
        o_ref[:, h * dv:(h + 1) * dv] = out.astype(BF16)


def _retention(proj, d, batch, seq):
    t = proj.shape[0]
    heads = RET_HEADS
    ch = min(RET_KERNEL_CHUNK, seq)
    nc = seq // ch
    dqk = d // 2 // heads
    dv = d // heads
    log_gamma = jnp.log(1.0 - 2.0 ** (-5.0 - jnp.arange(heads, dtype=F32)))
    idx = jnp.arange(ch, dtype=F32)
    rel = idx[:, None] - idx[None, :]
    dec = jnp.where(rel[None] >= 0,
                    jnp.exp(jnp.maximum(rel, 0.0)[None] * log_gamma[:, None, None]), 0.0)
    xi = jnp.exp((idx + 1.0)[None, :] * log_gamma[:, None])[:, :, None]
    zeta = jnp.exp((ch - 1.0 - idx)[None, :] * log_gamma[:, None])[:, :, None]
    gam = jnp.exp(ch * log_gamma)[:, None, None]
    row = lambda b, c: b * nc + c
    return pl.pallas_call(
        functools.partial(_ret_kernel, heads=heads),
        out_shape=jax.ShapeDtypeStruct((t, d), BF16),
        grid=(batch, nc),
        in_specs=[
            pl.BlockSpec((ch, d // 2), lambda b, c: (row(b, c), 0)),
            pl.BlockSpec((ch, d // 2), lambda b, c: (row(b, c), 1)),
            pl.BlockSpec((ch, d), lambda b, c: (row(b, c), 1)),
            pl.BlockSpec((heads, ch, ch), lambda b, c: (0, 0, 0)),
            pl.BlockSpec((heads, ch, 1), lambda b, c: (0, 0, 0)),
            pl.BlockSpec((heads, ch, 1), lambda b, c: (0, 0, 0)),
            pl.BlockSpec((heads, 1, 1), lambda b, c: (0, 0, 0)),
        ],
        out_specs=pl.BlockSpec((ch, d), lambda b, c: (row(b, c), 0)),
        scratch_shapes=[pltpu.VMEM((heads, dqk, dv), F32)],
        compiler_params=_params(("parallel", "arbitrary")),
        name="retention",
    )(proj, proj, proj, dec, xi, zeta, gam)


def _mix_kernel(*refs, cn, ncast, heads):
    (ret_ref, g_ref, yg_ref, ga_ref, gb_ref, h_ref, wo_ref, wv_ref, wg_ref,
     wout_ref) = refs[:10]
    cast_in = refs[10:10 + ncast]
    o_ref = refs[10 + ncast]
    cast_out = refs[11 + ncast:]
    _run_casts(cast_in, cast_out)
    d = h_ref.shape[1]
    dv = d // heads
    blocks = [slice(j * cn, (j + 1) * cn) for j in range(d // cn)]
    yg = jnp.concatenate([yg_ref[k] for k in range(yg_ref.shape[0])], axis=-1).astype(BF16)
    y_a = [jax.nn.sigmoid(ga_ref[:, c].astype(F32)) * _dot(yg, wv_ref[:, c])
           * jax.nn.sigmoid(_dot(yg, wg_ref[:, c])) for c in blocks]
    og = []
    for hd in range(heads):
        cols = slice(hd * dv, (hd + 1) * dv)
        o = ret_ref[:, cols].astype(F32)
        cen = o - jnp.mean(o, axis=-1, keepdims=True)
        var = jnp.mean(cen * cen, axis=-1, keepdims=True)
        g = g_ref[:, cols].astype(F32)
        og.append((g * jax.nn.sigmoid(g) * (cen * lax.rsqrt(var + GN_EPS))).astype(BF16))
    og = jnp.concatenate(og, axis=-1)
    merged = [(y_a[j] + jax.nn.sigmoid(gb_ref[:, c].astype(F32)) * _dot(og, wo_ref[:, c]))
              .astype(BF16) for j, c in enumerate(blocks)]
    o_ref[...] = h_ref[...] + _dot(jnp.concatenate(merged, axis=-1), wout_ref[...])


def _mix(ret, yg, proj, h, w_o, w_v, w_g, w_out, casts=(), *, tm=256, cn=256):
    t, d = h.shape
    nlb = yg.shape[0]
    sw = nlb * LANES
    tm, cn = min(tm, t), min(cn, d)
    cast_specs = [pl.BlockSpec(*_cast_block(c.shape, t // tm, 1)) for c in casts]
    return pl.pallas_call(
        functools.partial(_mix_kernel, cn=cn, ncast=len(casts), heads=RET_HEADS),
        out_shape=[jax.ShapeDtypeStruct((t, d), F32)]
        + [jax.ShapeDtypeStruct(c.shape, BF16) for c in casts],
        grid=(t // tm, 1),
        in_specs=[
            pl.BlockSpec((tm, d), lambda i, j: (i, 0)),
            pl.BlockSpec((tm, d), lambda i, j: (i, 2)),
            pl.BlockSpec((nlb, tm, LANES), lambda i, j: (0, i, 0)),
            pl.BlockSpec((tm, d), lambda i, j: (i, 3)),
            pl.BlockSpec((tm, d), lambda i, j: (i, 4)),
            pl.BlockSpec((tm, d), lambda i, j: (i, 0)),
            _resident((d, d)), _resident((sw, d)), _resident((sw, d)), _resident((d, d)),
        ] + cast_specs,
        out_specs=[pl.BlockSpec((tm, d), lambda i, j: (i, 0))] + cast_specs,
        compiler_params=_params(("arbitrary", "arbitrary")),
        name="mix",
    )(ret, proj, yg, proj, proj, h, w_o, w_v, w_g, w_out, *casts)


def _xattn_kernel(h_ref, g_ref, wq_ref, k_ref, v_ref, wo_ref, o_ref, *, heads):
    h = h_ref[...]
    d = h.shape[1]
    dh = d // heads
    xn = _rms(h, g_ref[...]).astype(BF16)
    outs = []
    for hd in range(heads):
        cols = slice(hd * dh, (hd + 1) * dh)
        q = _dot(xn, wq_ref[:, cols])
        s = _dot_nt(q.astype(BF16), k_ref[0, :, cols]) * (dh ** -0.5)
        e = jnp.exp(s - jnp.max(s, axis=-1, keepdims=True))
        p = e / jnp.sum(e, axis=-1, keepdims=True)
        outs.append(_dot(p.astype(BF16), v_ref[0, :, cols]).astype(BF16))
    o_ref[...] = h + _dot(jnp.concatenate(outs, axis=-1), wo_ref[...])


def _xattn(h, gain, wq, kmem, vmem, wo, batch, *, tm=1024):
    t, d = h.shape
    seq = t // batch
    mlen = kmem.shape[1]
    tm = min(tm, seq)
    nt = seq // tm
    return pl.pallas_call(
        functools.partial(_xattn_kernel, heads=XATTN_HEADS),
        out_shape=jax.ShapeDtypeStruct((t, d), F32),
        grid=(batch, nt),
        in_specs=[
            pl.BlockSpec((tm, d), lambda b, i: (b * nt + i, 0)),
            _resident((1, d)),
            _resident((d, d)),
            pl.BlockSpec((1, mlen, d), lambda b, i: (b, 0, 0)),
            pl.BlockSpec((1, mlen, d), lambda b, i: (b, 0, 0)),
            _resident((d, d)),
        ],
        out_specs=pl.BlockSpec((tm, d), lambda b, i: (b * nt + i, 0)),
        compiler_params=_params(("arbitrary", "arbitrary")),
        name="xattn",
    )(h, gain.reshape(1, d), wq, kmem, vmem, wo)


def kernel(x, mem, ffn1_norm, ffn1_w1, ffn1_w3, ffn1_w2, mix_norm, w_in, s5_a_re, s5_a_im, s5_log_dt, s5_b_re, s5_b_im, s5_c_re, s5_c_im, s5_d, s5_glu_v, s5_glu_g, ret_w_o, w_out, xattn_norm, mem_norm, xattn_wq, xattn_wk, xattn_wv, xattn_wo, ffn2_norm, ffn2_w1, ffn2_w3, ffn2_w2, final_norm):
    batch, seq, d = x.shape
    mlen = mem.shape[1]
    depth = ffn1_w1.shape[0]
    t = batch * seq
    s5_width = s5_d.shape[1]
    bf = lambda w: w.astype(BF16)

    h = x.reshape(t, d)
    mem2 = mem.reshape(batch * mlen, d)
    for l in range(depth):
        last = l == depth - 1
        later = (w_in, s5_glu_v, s5_glu_g, ret_w_o, w_out, xattn_wq, xattn_wk, xattn_wv, xattn_wo)
        h, c_in, c_gv, c_gg, c_ro, c_out, c_wq, c_wk, c_wv, c_wo = _ffn(
            h, ffn1_norm[l], bf(ffn1_w1[l]), bf(ffn1_w3[l]), bf(ffn1_w2[l]),
            casts=[w[l] for w in later])
        proj, us5 = _inproj(h, mix_norm[l], c_in, s5_width, seq)
        toep, s_in, s_out, lam, dtile = _s5_weights(
            s5_a_re[l], s5_a_im[l], s5_log_dt[l], s5_b_re[l], s5_b_im[l],
            s5_c_re[l], s5_c_im[l], s5_d[l])
        yg = _s5(us5, toep, s_in, s_out, lam, dtile, batch)
        ret = _retention(proj, d, batch, seq)
        h, c_f1, c_f3, c_f2 = _mix(ret, yg, proj, h, c_ro, c_gv, c_gg, c_out,
                                   casts=[ffn2_w1[l], ffn2_w3[l], ffn2_w2[l]])
        kmem, vmem = _normmm(mem2, mem_norm[l], c_wk, c_wv)
        h = _xattn(h, xattn_norm[l], c_wq, kmem.reshape(batch, mlen, d),
                   vmem.reshape(batch, mlen, d), c_wo, batch)
        h, = _ffn(h, ffn2_norm[l], c_f1, c_f3, c_f2, final_norm if last else None)
    if depth == 0:
        raise ValueError("depth must be >= 1")
    return h.reshape(batch, seq, d)
```

```python
import functools

import jax
import jax.numpy as jnp
from jax import lax
from jax.experimental import pallas as pl
from jax.experimental.pallas import tpu as pltpu

F32 = jnp.float32
BF16 = jnp.bfloat16

RMS_EPS = 1e-6
GN_EPS = 1e-5
ROPE_BASE = 10000.0
S5_GROUP = 16
RET_HEADS = 4
XATTN_HEADS = 4
RET_KERNEL_CHUNK = 256
S5_CHUNK = 8
LANES = 128
GROUPS_PER_LANE_BLOCK = LANES // S5_GROUP
VMEM_LIMIT_BYTES = 60 * 1024 * 1024
MXU_TILE = 256
SCAN_STRIDE = 2


def _params(semantics):
    return pltpu.CompilerParams(dimension_semantics=semantics,
                                vmem_limit_bytes=VMEM_LIMIT_BYTES)


def _resident(shape):
    return pl.BlockSpec(shape, lambda *_: (0,) * len(shape), pipeline_mode=pl.Buffered(1))


def _rms(x, gain):
    ms = jnp.mean(x * x, axis=-1, keepdims=True)
    return x * lax.rsqrt(ms + RMS_EPS) * gain


def _dot(a, b):
    return jnp.dot(a, b, preferred_element_type=F32)


def _dot_nt(a, b):
    return lax.dot_general(a, b, (((1,), (1,)), ((), ())), preferred_element_type=F32)


def _dot_tn(a, b):
    return lax.dot_general(a, b, (((0,), (0,)), ((), ())), preferred_element_type=F32)


def _run_casts(cast_in, cast_out):
    for src_ref, dst_ref in zip(cast_in, cast_out):
        dst_ref[...] = src_ref[...].astype(BF16)


def _ffn_kernel(*refs, final, ncast):
    nin = 6 if final else 5
    x_ref, g_ref, w1_ref, w3_ref, w2_ref = refs[:5]
    fg_ref = refs[5] if final else None
    cast_in = refs[nin:nin + ncast]
    o_ref = refs[nin + ncast]
    cast_out = refs[nin + ncast + 1:nin + 2 * ncast + 1]
    xn_ref = refs[-1]
    j = pl.program_id(1)

    _run_casts(cast_in, cast_out)

    last = pl.num_programs(1) - 1

    def half_swiglu(xn):
        a = _dot(xn, w1_ref[...])
        b = _dot(xn, w3_ref[...])
        mid = (a * jax.nn.sigmoid(a) * b * 0.5).astype(BF16)
        return _dot(mid, w2_ref[...])

    @pl.when(j == 0)
    def _():
        x = x_ref[...]
        xn = _rms(x, g_ref[...]).astype(BF16)
        xn_ref[...] = xn
        o_ref[...] = x + half_swiglu(xn)

    @pl.when(jnp.logical_and(j > 0, j < last) if final else j > 0)
    def _():
        o_ref[...] += half_swiglu(xn_ref[...])

    if final:
        @pl.when(j == last)
        def _():
            o_ref[...] = _rms(o_ref[...] + half_swiglu(xn_ref[...]), fg_ref[...])


def _first_step_only(nt):
    return lambda i, j: (jnp.minimum(i + jnp.minimum(j, 1), nt - 1), 0)


def _cast_block(shape, nt, nf):
    r, c = shape

    def tiled(rows, cols):
        return rows % 16 == 0 and cols % LANES == 0

    if r % nt == 0 and c % nf == 0 and tiled(r // nt, c // nf):
        return (r // nt, c // nf), lambda i, j: (i, j)
    if r % nf == 0 and c % nt == 0 and tiled(r // nf, c // nt):
        return (r // nf, c // nt), lambda i, j: (j, i)
    nrow = max(n for n in range(1, nt + 1) if r % n == 0 and (r // n) % 16 == 0)
    ncol = max(n for n in range(1, nf + 1) if c % n == 0 and (c // n) % LANES == 0)
    return ((r // nrow, c // ncol),
            lambda i, j: (jnp.minimum(i, nrow - 1), jnp.minimum(j, ncol - 1)))


def _ffn(x, gain, w1, w3, w2, final_gain=None, casts=(), *, tm=1024, tf=512):
    t, d = x.shape
    f = w1.shape[1]
    tm, tf = min(tm, t), min(tf, f)
    final = final_gain is not None
    in_specs = [
        pl.BlockSpec((tm, d), _first_step_only(t // tm)),
        pl.BlockSpec((1, d), lambda i, j: (0, 0)),
        pl.BlockSpec((d, tf), lambda i, j: (0, j)),
        pl.BlockSpec((d, tf), lambda i, j: (0, j)),
        pl.BlockSpec((tf, d), lambda i, j: (j, 0)),
    ]
    args = [x, gain.reshape(1, d), w1, w3, w2]
    if final:
        in_specs.append(pl.BlockSpec((1, d), lambda i, j: (0, 0)))
        args.append(final_gain.reshape(1, d))
    cast_specs = [pl.BlockSpec(*_cast_block(w.shape, t // tm, f // tf)) for w in casts]
    outs = pl.pallas_call(
        functools.partial(_ffn_kernel, final=final, ncast=len(casts)),
        out_shape=[jax.ShapeDtypeStruct((t, d), F32)]
        + [jax.ShapeDtypeStruct(w.shape, BF16) for w in casts],
        grid=(t // tm, f // tf),
        in_specs=in_specs + cast_specs,
        out_specs=[pl.BlockSpec((tm, d), lambda i, j: (i, 0))] + cast_specs,
        scratch_shapes=[pltpu.VMEM((tm, d), BF16)],
        compiler_params=_params(("arbitrary", "arbitrary")),
        name="ffn_final" if final else "ffn",
    )(*args, *casts)
    return outs


def _rotary(t, cos, sin, heads):
    dh = t.shape[1] // heads
    half = dh // 2
    out = []
    for h in range(heads):
        t1, t2 = t[:, h * dh:h * dh + half], t[:, h * dh + half:(h + 1) * dh]
        out += [t1 * cos - t2 * sin, t1 * sin + t2 * cos]
    return jnp.concatenate(out, axis=-1)


def _inproj_kernel(x_ref, g_ref, wa_ref, wb_ref, cos_ref, sin_ref, proj_ref, us5_ref, xn_ref,
                   *, heads):
    j = pl.program_id(1)
    tn = wa_ref.shape[1]

    @pl.when(j == 0)
    def _():
        xn = _rms(x_ref[...], g_ref[...]).astype(BF16)
        xn_ref[...] = xn
        res = _dot(xn, wa_ref[...])
        for k in range(us5_ref.shape[0]):
            us5_ref[k] = res[:, k * LANES:(k + 1) * LANES]

    @pl.when(j == 1)
    def _():
        xn = xn_ref[...]
        cos, sin = cos_ref[...], sin_ref[...]
        q = _rotary(_dot(xn, wa_ref[...]), cos, sin, heads)
        k = _rotary(_dot(xn, wb_ref[...]), cos, sin, heads) * ((tn // heads) ** -0.5)
        proj_ref[:, :tn] = q.astype(BF16)
        proj_ref[:, tn:] = k.astype(BF16)

    @pl.when(j > 1)
    def _():
        xn = xn_ref[...]
        proj_ref[:, :tn] = _dot(xn, wa_ref[...]).astype(BF16)
        proj_ref[:, tn:] = _dot(xn, wb_ref[...]).astype(BF16)


def _rope_tables(seq, dh):
    inv = ROPE_BASE ** (-jnp.arange(0, dh, 2, dtype=F32) / dh)
    ang = jnp.arange(seq, dtype=F32)[:, None] * inv[None, :]
    return jnp.cos(ang), jnp.sin(ang)


def _inproj(x, gain, w, s5_width, seq, *, tm=1024):
    t, d = x.shape
    n = w.shape[1]
    tn = s5_width
    tm = min(tm, seq)
    assert seq % tm == 0 and tn == d // 2, (seq, tm, tn, d)
    pairs, odd = divmod(n // tn - 1, 2)
    assert odd == 0, (n, tn)
    nt, nf = t // tm, 1 + pairs
    half = tn // RET_HEADS // 2
    cos, sin = _rope_tables(seq, 2 * half)
    pos = pl.BlockSpec((tm, half), lambda i, j: (i % (seq // tm), 0))
    return pl.pallas_call(
        functools.partial(_inproj_kernel, heads=RET_HEADS),
        out_shape=[jax.ShapeDtypeStruct((t, n - tn), BF16),
                   jax.ShapeDtypeStruct((tn // LANES, t, LANES), F32)],
        grid=(nt, nf),
        in_specs=[
            pl.BlockSpec((tm, d), _first_step_only(nt)),
            pl.BlockSpec((1, d), lambda i, j: (0, 0)),
            pl.BlockSpec((d, tn), lambda i, j: (0, jnp.maximum(2 * j - 1, 0))),
            pl.BlockSpec((d, tn), lambda i, j: (0, 2 * jnp.maximum(j, 1))),
            pos, pos,
        ],
        out_specs=[
            pl.BlockSpec((tm, 2 * tn), lambda i, j: (i, jnp.maximum(j - 1, 0))),
            pl.BlockSpec((tn // LANES, tm, LANES), lambda i, j: (0, i, 0)),
        ],
        scratch_shapes=[pltpu.VMEM((tm, d), BF16)],
        compiler_params=_params(("arbitrary", "arbitrary")),
        name="inproj",
    )(x, gain.reshape(1, d), w, w, cos, sin)


def _normmm_kernel(x_ref, g_ref, wk_ref, wv_ref, k_ref, v_ref, xn_ref):
    @pl.when(pl.program_id(0) == 0)
    def _():
        xn_ref[...] = _rms(x_ref[...], g_ref[...]).astype(BF16)

    xn = xn_ref[...]
    k_ref[...] = _dot(xn, wk_ref[...]).astype(BF16)
    v_ref[...] = _dot(xn, wv_ref[...]).astype(BF16)


def _normmm(x, gain, wk, wv, *, tn=1024):
    t, d = x.shape
    n = wk.shape[1]
    tn = min(tn, n)
    col = lambda rows: pl.BlockSpec((rows, tn), lambda j: (0, j))
    return pl.pallas_call(
        _normmm_kernel,
        out_shape=[jax.ShapeDtypeStruct((t, n), BF16)] * 2,
        grid=(n // tn,),
        in_specs=[_resident((t, d)), _resident((1, d)), col(d), col(d)],
        out_specs=[col(t), col(t)],
        scratch_shapes=[pltpu.VMEM((t, d), BF16)],
        compiler_params=_params(("arbitrary",)),
        name="normmm",
    )(x, gain.reshape(1, d), wk, wv)


def _s5_kernel(u_ref, m_ref, win_ref, wout_ref, lam_ref, d_ref, y_ref, z_ref, s_ref, *, tc, nb):
    rb = z_ref.shape[0] // nb
    seq = rb * tc
    u32 = jnp.concatenate(
        [jnp.concatenate([u_ref[pl.ds(b * seq + j, rb, stride=tc), :] for j in range(tc)], axis=-1)
         for b in range(nb)], axis=0)
    u = u32.astype(BF16)
    z_ref[...] = _dot(u, win_ref[0])
    ns = z_ref.shape[1] // 2
    lam_re = lam_ref[0, :, :ns]
    lam_im = lam_ref[0, :, ns:]

    per_tile = MXU_TILE // LANES
    blocks = [slice(cb * MXU_TILE, (cb + 1) * MXU_TILE) for cb in range(tc // per_tile)]
    y_intra = [_dot(u[:, :(cb + 1) * MXU_TILE], m_ref[0, :(cb + 1) * MXU_TILE, cols])
               + d_ref[0, :, cols] * u32[:, cols] for cb, cols in enumerate(blocks)]

    def cmul_add(a_re, a_im, x_re, x_im, y_re, y_im):
        return a_re * x_re - a_im * x_im + y_re, a_re * x_im + a_im * x_re + y_im

    stride = SCAN_STRIDE
    assert rb % stride == 0, (rb, stride)
    zero = jnp.zeros((1, ns), F32)
    lamk = (lam_re, lam_im)
    for _ in range(stride - 1):
        lamk = cmul_add(lam_re, lam_im, *lamk, zero, zero)
    for b in range(nb):
        s = (zero, zero)
        for c in range(0, rb, stride):
            rows = [slice(b * rb + c + i, b * rb + c + i + 1) for i in range(stride)]
            z = [(z_ref[r, :ns], z_ref[r, ns:]) for r in rows]
            t, w = s, z[0]
            for i, r in enumerate(rows):
                s_ref[r, :ns], s_ref[r, ns:] = t
                if i + 1 < stride:
                    t = cmul_add(lam_re, lam_im, *t, *z[i])
                    w = cmul_add(lam_re, lam_im, *w, *z[i + 1])
            s = cmul_add(*lamk, *s, *w)

    s = s_ref[...].astype(BF16)
    for cb, cols in enumerate(blocks):
        y = jax.nn.gelu(y_intra[cb] + _dot(s, wout_ref[0, :, cols]))
        for b in range(nb):
            for i in range(per_tile):
                y_ref[pl.ds(b * seq + cb * per_tile + i, rb, stride=tc), :] = (
                    y[b * rb:(b + 1) * rb, i * LANES:(i + 1) * LANES])


def _s5(u, m, w_in, w_out, lam, dtile, batch, *, nb=1):
    nlb, t, _ = u.shape
    seq = t // batch
    tc = S5_CHUNK
    _, width, nstate = w_in.shape
    nb = nb if batch % nb == 0 else 1
    rows = nb * seq // tc
    return pl.pallas_call(
        functools.partial(_s5_kernel, tc=tc, nb=nb),
        out_shape=jax.ShapeDtypeStruct((nlb, t, LANES), F32),
        grid=(nlb, batch // nb),
        in_specs=[
            pl.BlockSpec((None, nb * seq, LANES), lambda k, b: (k, b, 0)),
            pl.BlockSpec((1, width, width), lambda k, b: (k, 0, 0)),
            pl.BlockSpec((1, width, nstate), lambda k, b: (k, 0, 0)),
            pl.BlockSpec((1, nstate, width), lambda k, b: (k, 0, 0)),
            pl.BlockSpec((1, 1, nstate), lambda k, b: (k, 0, 0)),
            pl.BlockSpec((1, 1, width), lambda k, b: (k, 0, 0)),
        ],
        out_specs=pl.BlockSpec((None, nb * seq, LANES), lambda k, b: (k, b, 0)),
        scratch_shapes=[pltpu.VMEM((rows, nstate), F32), pltpu.VMEM((rows, nstate), F32)],
        compiler_params=_params(("parallel", "arbitrary")),
        name="s5",
    )(u, m, w_in, w_out, lam, dtile)


def _log2(n):
    assert n & (n - 1) == 0, n
    return n.bit_length() - 1


def _s5prep_kernel(bre_ref, bim_ref, cre_ref, cim_ref, lrow_ref, lcol_ref,
                   toep_ref, win_ref, wout_ref, lam_ref, *, tc):
    ns, gi = bre_ref.shape[1], bre_ref.shape[2]
    lanes, p = cre_ref.shape[1], cre_ref.shape[2]

    def iota(shape, dim):
        return lax.broadcasted_iota(jnp.int32, shape, dim)

    def grp(x, size):
        return lax.shift_right_logical(x, _log2(size))

    def split(a):
        hi = a.astype(BF16)
        return hi, (a - hi.astype(F32)).astype(BF16)

    def dot3(a, b):
        return _dot(a[0], b[0]) + _dot(a[0], b[1]) + _dot(a[1], b[0])

    e_i = ((iota((gi, lanes), 1) & (gi - 1)) == iota((gi, lanes), 0)).astype(BF16)
    e_p = ((iota((p, ns), 1) & (p - 1)) == iota((p, ns), 0)).astype(BF16)

    def repeat(a, e):
        hi, lo = split(a)
        return _dot(hi, e) + _dot(lo, e)

    mask_b = grp(iota((ns, lanes), 0), p) == grp(iota((ns, lanes), 1), gi)
    mask_c = grp(iota((lanes, ns), 0), gi) == grp(iota((lanes, ns), 1), p)

    def bdiag_b(ref):
        return jnp.where(mask_b, repeat(ref[0], e_i), 0.0).T

    def bdiag_c(ref):
        return jnp.where(mask_c, repeat(ref[0], e_p), 0.0).T

    b_re, b_im = bdiag_b(bre_ref), bdiag_b(bim_ref)
    c_re, c_im = bdiag_c(cre_ref), bdiag_c(cim_ref)
    c_re2, c_im2 = split(c_re), split(c_im)
    l_re, l_im = lrow_ref[0, 0:1, :], lrow_ref[0, 1:2, :]
    lc_re, lc_im = lcol_ref[0, :, 0:1], lcol_ref[0, :, 1:2]
    p_re, p_im = jnp.ones_like(l_re), jnp.zeros_like(l_im)
    q_re, q_im = jnp.ones_like(lc_re), jnp.zeros_like(lc_im)
    zero_tile = jnp.zeros((lanes, lanes), BF16)

    def tile(i):
        return slice(i * lanes, (i + 1) * lanes)

    for n in range(tc):
        lb_re = b_re * p_re - b_im * p_im
        lb_im = b_re * p_im + b_im * p_re
        win_ref[0, tile(tc - 1 - n), :ns] = lb_re.astype(BF16)
        win_ref[0, tile(tc - 1 - n), ns:] = lb_im.astype(BF16)
        kern = (dot3(split(lb_re), c_re2) - dot3(split(lb_im), c_im2)).astype(BF16)
        for j in range(tc - n):
            toep_ref[0, tile(j), tile(j + n)] = kern
        if n:
            for t in range(tc - n):
                toep_ref[0, tile(t + n), tile(t)] = zero_tile
        p_re, p_im = p_re * l_re - p_im * l_im, p_re * l_im + p_im * l_re
        q_re, q_im = q_re * lc_re - q_im * lc_im, q_re * lc_im + q_im * lc_re
        wout_ref[0, :ns, tile(n)] = (c_re * q_re - c_im * q_im).astype(BF16)
        wout_ref[0, ns:, tile(n)] = (-(c_re * q_im + c_im * q_re)).astype(BF16)
    lam_ref[0, :, :ns] = p_re
    lam_ref[0, :, ns:] = p_im


def _s5_weights(a_re, a_im, log_dt, b_re, b_im, c_re, c_im, d_skip):
    tc = S5_CHUNK
    g, p = a_re.shape
    gl = GROUPS_PER_LANE_BLOCK
    nlb = g // gl
    ns = gl * p
    dt = jnp.exp(log_dt)[:, None]
    mag = jnp.exp(a_re * dt)
    l_re = mag * jnp.cos(a_im * dt)
    l_im = mag * jnp.sin(a_im * dt)
    den = a_re * a_re + a_im * a_im
    n_re = l_re - 1.0
    n_im = l_im
    f_re = (n_re * a_re + n_im * a_im) / den
    f_im = (n_im * a_re - n_re * a_im) / den
    bb_re = (f_re[..., None] * b_re - f_im[..., None] * b_im).reshape(nlb, ns, S5_GROUP)
    bb_im = (f_re[..., None] * b_im + f_im[..., None] * b_re).reshape(nlb, ns, S5_GROUP)
    lrow = jnp.stack([l_re.reshape(nlb, ns), l_im.reshape(nlb, ns)], axis=1)
    lcol = jnp.stack([l_re.reshape(nlb, ns), l_im.reshape(nlb, ns)], axis=2)
    width = tc * LANES
    blk = lambda *shape: pl.BlockSpec((1,) + shape, lambda k: (k, 0, 0))
    toep, w_in, w_out, lam = pl.pallas_call(
        functools.partial(_s5prep_kernel, tc=tc),
        out_shape=(jax.ShapeDtypeStruct((nlb, width, width), BF16),
                   jax.ShapeDtypeStruct((nlb, width, 2 * ns), BF16),
                   jax.ShapeDtypeStruct((nlb, 2 * ns, width), BF16),
                   jax.ShapeDtypeStruct((nlb, 1, 2 * ns), F32)),
        grid=(nlb,),
        in_specs=[blk(ns, S5_GROUP), blk(ns, S5_GROUP), blk(LANES, p), blk(LANES, p),
                  blk(2, ns), blk(ns, 2)],
        out_specs=(blk(width, width), blk(width, 2 * ns), blk(2 * ns, width),
                   blk(1, 2 * ns)),
        compiler_params=_params(("parallel",)),
        name="s5prep",
    )(bb_re, bb_im, c_re.reshape(nlb, LANES, p), c_im.reshape(nlb, LANES, p), lrow, lcol)
    dtile = jnp.tile(d_skip.reshape(nlb, 1, LANES), (1, 1, tc))
    return toep, w_in, w_out, lam, dtile


def _ret_kernel(q_ref, k_ref, v_ref, dec_ref, xi_ref, zeta_ref, gam_ref, o_ref, state_ref,
                *, heads):
    c = pl.program_id(1)

    @pl.when(c == 0)
    def _():
        state_ref[...] = jnp.zeros_like(state_ref)

    dqk = q_ref.shape[2] // heads
    dv = v_ref.shape[2] // heads
    for b in range(q_ref.shape[0]):
        for h in range(heads):
            qb = q_ref[b, :, h * dqk:(h + 1) * dqk]
            kb = k_ref[b, :, h * dqk:(h + 1) * dqk]
            v = v_ref[b, :, h * dv:(h + 1) * dv]
            scores = _dot_nt(qb, kb) * dec_ref[h]
            st = state_ref[b, h]
            out = _dot(scores.astype(BF16), v) + _dot(qb, st.astype(BF16)) * xi_ref[h]
            kz = (kb.astype(F32) * zeta_ref[h]).astype(BF16)
            state_ref[b, h] = st * gam_ref[h] + _dot_tn(kz, v)
            o_ref[b, :, h * dv:(h + 1) * dv] = out.astype(BF16)


def _retention(proj, d, batch, seq, *, nb=2):
    t = proj.shape[0]
    heads = RET_HEADS
    nb = nb if batch % nb == 0 else 1
    proj3 = proj.reshape(batch, seq, proj.shape[1])
    ch = min(RET_KERNEL_CHUNK, seq)
    nc = seq // ch
    dqk = d // 2 // heads
    dv = d // heads
    log_gamma = jnp.log(1.0 - 2.0 ** (-5.0 - jnp.arange(heads, dtype=F32)))
    idx = jnp.arange(ch, dtype=F32)
    rel = idx[:, None] - idx[None, :]
    dec = jnp.where(rel[None] >= 0,
                    jnp.exp(jnp.maximum(rel, 0.0)[None] * log_gamma[:, None, None]), 0.0)
    xi = jnp.exp((idx + 1.0)[None, :] * log_gamma[:, None])[:, :, None]
    zeta = jnp.exp((ch - 1.0 - idx)[None, :] * log_gamma[:, None])[:, :, None]
    gam = jnp.exp(ch * log_gamma)[:, None, None]
    return pl.pallas_call(
        functools.partial(_ret_kernel, heads=heads),
        out_shape=jax.ShapeDtypeStruct((batch, seq, d), BF16),
        grid=(batch // nb, nc),
        in_specs=[
            pl.BlockSpec((nb, ch, d // 2), lambda b, c: (b, c, 0)),
            pl.BlockSpec((nb, ch, d // 2), lambda b, c: (b, c, 1)),
            pl.BlockSpec((nb, ch, d), lambda b, c: (b, c, 1)),
            pl.BlockSpec((heads, ch, ch), lambda b, c: (0, 0, 0)),
            pl.BlockSpec((heads, ch, 1), lambda b, c: (0, 0, 0)),
            pl.BlockSpec((heads, ch, 1), lambda b, c: (0, 0, 0)),
            pl.BlockSpec((heads, 1, 1), lambda b, c: (0, 0, 0)),
        ],
        out_specs=pl.BlockSpec((nb, ch, d), lambda b, c: (b, c, 0)),
        scratch_shapes=[pltpu.VMEM((nb, heads, dqk, dv), F32)],
        compiler_params=_params(("parallel", "arbitrary")),
        name="retention",
    )(proj3, proj3, proj3, dec, xi, zeta, gam).reshape(t, d)


def _mix_kernel(*refs, cn, ncast, heads):
    (ret_ref, g_ref, yg_ref, ga_ref, gb_ref, h_ref, wo_ref, wv_ref, wg_ref,
     wout_ref) = refs[:10]
    cast_in = refs[10:10 + ncast]
    o_ref = refs[10 + ncast]
    cast_out = refs[11 + ncast:]
    _run_casts(cast_in, cast_out)
    d = h_ref.shape[1]
    dv = d // heads
    blocks = [slice(j * cn, (j + 1) * cn) for j in range(d // cn)]
    yg = jnp.concatenate([yg_ref[k] for k in range(yg_ref.shape[0])], axis=-1).astype(BF16)
    y_a = [jax.nn.sigmoid(ga_ref[:, c].astype(F32)) * _dot(yg, wv_ref[:, c])
           * jax.nn.sigmoid(_dot(yg, wg_ref[:, c])) for c in blocks]
    og = []
    for hd in range(heads):
        cols = slice(hd * dv, (hd + 1) * dv)
        o = ret_ref[:, cols].astype(F32)
        cen = o - jnp.mean(o, axis=-1, keepdims=True)
        var = jnp.mean(cen * cen, axis=-1, keepdims=True)
        g = g_ref[:, cols].astype(F32)
        og.append((g * jax.nn.sigmoid(g) * (cen * lax.rsqrt(var + GN_EPS))).astype(BF16))
    og = jnp.concatenate(og, axis=-1)
    merged = [(y_a[j] + jax.nn.sigmoid(gb_ref[:, c].astype(F32)) * _dot(og, wo_ref[:, c]))
              .astype(BF16) for j, c in enumerate(blocks)]
    o_ref[...] = h_ref[...] + _dot(jnp.concatenate(merged, axis=-1), wout_ref[...])


def _mix(ret, yg, proj, h, w_o, w_v, w_g, w_out, casts=(), *, tm=256, cn=256):
    t, d = h.shape
    nlb = yg.shape[0]
    sw = nlb * LANES
    tm, cn = min(tm, t), min(cn, d)
    cast_specs = [pl.BlockSpec(*_cast_block(c.shape, t // tm, 1)) for c in casts]
    return pl.pallas_call(
        functools.partial(_mix_kernel, cn=cn, ncast=len(casts), heads=RET_HEADS),
        out_shape=[jax.ShapeDtypeStruct((t, d), F32)]
        + [jax.ShapeDtypeStruct(c.shape, BF16) for c in casts],
        grid=(t // tm, 1),
        in_specs=[
            pl.BlockSpec((tm, d), lambda i, j: (i, 0)),
            pl.BlockSpec((tm, d), lambda i, j: (i, 2)),
            pl.BlockSpec((nlb, tm, LANES), lambda i, j: (0, i, 0)),
            pl.BlockSpec((tm, d), lambda i, j: (i, 3)),
            pl.BlockSpec((tm, d), lambda i, j: (i, 4)),
            pl.BlockSpec((tm, d), lambda i, j: (i, 0)),
            _resident((d, d)), _resident((sw, d)), _resident((sw, d)), _resident((d, d)),
        ] + cast_specs,
        out_specs=[pl.BlockSpec((tm, d), lambda i, j: (i, 0))] + cast_specs,
        compiler_params=_params(("arbitrary", "arbitrary")),
        name="mix",
    )(ret, proj, yg, proj, proj, h, w_o, w_v, w_g, w_out, *casts)


def _xattn_kernel(h_ref, g_ref, wq_ref, k_ref, v_ref, wo_ref, o_ref, *, heads):
    h = h_ref[...]
    d = h.shape[1]
    dh = d // heads
    xn = _rms(h, g_ref[...]).astype(BF16)
    outs = []
    for hd in range(heads):
        cols = slice(hd * dh, (hd + 1) * dh)
        q = _dot(xn, wq_ref[:, cols])
        s = _dot_nt(q.astype(BF16), k_ref[0, :, cols]) * (dh ** -0.5)
        e = jnp.exp(s - jnp.max(s, axis=-1, keepdims=True))
        p = e / jnp.sum(e, axis=-1, keepdims=True)
        outs.append(_dot(p.astype(BF16), v_ref[0, :, cols]).astype(BF16))
    o_ref[...] = h + _dot(jnp.concatenate(outs, axis=-1), wo_ref[...])


def _xattn(h, gain, wq, kmem, vmem, wo, batch, *, tm=1024):
    t, d = h.shape
    seq = t // batch
    mlen = kmem.shape[1]
    tm = min(tm, seq)
    nt = seq // tm
    return pl.pallas_call(
        functools.partial(_xattn_kernel, heads=XATTN_HEADS),
        out_shape=jax.ShapeDtypeStruct((t, d), F32),
        grid=(batch, nt),
        in_specs=[
            pl.BlockSpec((tm, d), lambda b, i: (b * nt + i, 0)),
            _resident((1, d)),
            _resident((d, d)),
            pl.BlockSpec((1, mlen, d), lambda b, i: (b, 0, 0)),
            pl.BlockSpec((1, mlen, d), lambda b, i: (b, 0, 0)),
            _resident((d, d)),
        ],
        out_specs=pl.BlockSpec((tm, d), lambda b, i: (b * nt + i, 0)),
        compiler_params=_params(("arbitrary", "arbitrary")),
        name="xattn",
    )(h, gain.reshape(1, d), wq, kmem, vmem, wo)


def kernel(x, mem, ffn1_norm, ffn1_w1, ffn1_w3, ffn1_w2, mix_norm, w_in, s5_a_re, s5_a_im, s5_log_dt, s5_b_re, s5_b_im, s5_c_re, s5_c_im, s5_d, s5_glu_v, s5_glu_g, ret_w_o, w_out, xattn_norm, mem_norm, xattn_wq, xattn_wk, xattn_wv, xattn_wo, ffn2_norm, ffn2_w1, ffn2_w3, ffn2_w2, final_norm):
    batch, seq, d = x.shape
    mlen = mem.shape[1]
    depth = ffn1_w1.shape[0]
    t = batch * seq
    s5_width = s5_d.shape[1]
    bf = lambda w: w.astype(BF16)

    h = x.reshape(t, d)
    mem2 = mem.reshape(batch * mlen, d)
    for l in range(depth):
        last = l == depth - 1
        later = (w_in, s5_glu_v, s5_glu_g, ret_w_o, w_out, xattn_wq, xattn_wk, xattn_wv, xattn_wo)
        h, c_in, c_gv, c_gg, c_ro, c_out, c_wq, c_wk, c_wv, c_wo = _ffn(
            h, ffn1_norm[l], bf(ffn1_w1[l]), bf(ffn1_w3[l]), bf(ffn1_w2[l]),
            casts=[w[l] for w in later])
        proj, us5 = _inproj(h, mix_norm[l], c_in, s5_width, seq)
        toep, s_in, s_out, lam, dtile = _s5_weights(
            s5_a_re[l], s5_a_im[l], s5_log_dt[l], s5_b_re[l], s5_b_im[l],
            s5_c_re[l], s5_c_im[l], s5_d[l])
        yg = _s5(us5, toep, s_in, s_out, lam, dtile, batch)
        ret = _retention(proj, d, batch, seq)
        h, c_f1, c_f3, c_f2 = _mix(ret, yg, proj, h, c_ro, c_gv, c_gg, c_out,
                                   casts=[ffn2_w1[l], ffn2_w3[l], ffn2_w2[l]])
        kmem, vmem = _normmm(mem2, mem_norm[l], c_wk, c_wv)
        h = _xattn(h, xattn_norm[l], c_wq, kmem.reshape(batch, mlen, d),
                   vmem.reshape(batch, mlen, d), c_wo, batch)
        h, = _ffn(h, ffn2_norm[l], c_f1, c_f3, c_f2, final_norm if last else None)
    if depth == 0:
        raise ValueError("depth must be >= 1")
    return h.reshape(batch, seq, d)
```

```python
import functools

import jax
import jax.numpy as jnp
from jax import lax
from jax.experimental import pallas as pl
from jax.experimental.pallas import tpu as pltpu

F32 = jnp.float32
BF16 = jnp.bfloat16

RMS_EPS = 1e-6
GN_EPS = 1e-5
ROPE_BASE = 10000.0
S5_GROUP = 16
RET_HEADS = 4
XATTN_HEADS = 4
RET_KERNEL_CHUNK = 256
S5_CHUNK = 8
LANES = 128
GROUPS_PER_LANE_BLOCK = LANES // S5_GROUP
VMEM_LIMIT_BYTES = 60 * 1024 * 1024
MXU_TILE = 256
SCAN_STRIDE = 2


def _params(semantics):
    return pltpu.CompilerParams(dimension_semantics=semantics,
                                vmem_limit_bytes=VMEM_LIMIT_BYTES)


def _resident(shape):
    return pl.BlockSpec(shape, lambda *_: (0,) * len(shape), pipeline_mode=pl.Buffered(1))


def _rms(x, gain):
    ms = jnp.mean(x * x, axis=-1, keepdims=True)
    return x * lax.rsqrt(ms + RMS_EPS) * gain


def _dot(a, b):
    return jnp.dot(a, b, preferred_element_type=F32)


def _dot_nt(a, b):
    return lax.dot_general(a, b, (((1,), (1,)), ((), ())), preferred_element_type=F32)


def _dot_tn(a, b):
    return lax.dot_general(a, b, (((0,), (0,)), ((), ())), preferred_element_type=F32)


def _run_casts(cast_in, cast_out):
    for src_ref, dst_ref in zip(cast_in, cast_out):
        dst_ref[...] = src_ref[...].astype(BF16)


def _ffn_kernel(*refs, final, ncast):
    nin = 6 if final else 5
    x_ref, g_ref, w1_ref, w3_ref, w2_ref = refs[:5]
    fg_ref = refs[5] if final else None
    cast_in = refs[nin:nin + ncast]
    o_ref = refs[nin + ncast]
    cast_out = refs[nin + ncast + 1:nin + 2 * ncast + 1]
    xn_ref = refs[-1]
    j = pl.program_id(1)

    _run_casts(cast_in, cast_out)

    last = pl.num_programs(1) - 1

    def half_swiglu(xn):
        a = _dot(xn, w1_ref[...])
        b = _dot(xn, w3_ref[...])
        mid = (a * jax.nn.sigmoid(a) * b * 0.5).astype(BF16)
        return _dot(mid, w2_ref[...])

    @pl.when(j == 0)
    def _():
        x = x_ref[...]
        xn = _rms(x, g_ref[...]).astype(BF16)
        xn_ref[...] = xn
        o_ref[...] = x + half_swiglu(xn)

    @pl.when(jnp.logical_and(j > 0, j < last) if final else j > 0)
    def _():
        o_ref[...] += half_swiglu(xn_ref[...])

    if final:
        @pl.when(j == last)
        def _():
            o_ref[...] = _rms(o_ref[...] + half_swiglu(xn_ref[...]), fg_ref[...])


def _first_step_only(nt):
    return lambda i, j: (jnp.minimum(i + jnp.minimum(j, 1), nt - 1), 0)


def _cast_block(shape, nt, nf):
    r, c = shape

    def tiled(rows, cols):
        return rows % 16 == 0 and cols % LANES == 0

    if r % nt == 0 and c % nf == 0 and tiled(r // nt, c // nf):
        return (r // nt, c // nf), lambda i, j: (i, j)
    if r % nf == 0 and c % nt == 0 and tiled(r // nf, c // nt):
        return (r // nf, c // nt), lambda i, j: (j, i)
    nrow = max(n for n in range(1, nt + 1) if r % n == 0 and (r // n) % 16 == 0)
    ncol = max(n for n in range(1, nf + 1) if c % n == 0 and (c // n) % LANES == 0)
    return ((r // nrow, c // ncol),
            lambda i, j: (jnp.minimum(i, nrow - 1), jnp.minimum(j, ncol - 1)))


def _ffn(x, gain, w1, w3, w2, final_gain=None, casts=(), *, tm=1024, tf=512):
    t, d = x.shape
    f = w1.shape[1]
    tm, tf = min(tm, t), min(tf, f)
    final = final_gain is not None
    in_specs = [
        pl.BlockSpec((tm, d), _first_step_only(t // tm)),
        pl.BlockSpec((1, d), lambda i, j: (0, 0)),
        pl.BlockSpec((d, tf), lambda i, j: (0, j)),
        pl.BlockSpec((d, tf), lambda i, j: (0, j)),
        pl.BlockSpec((tf, d), lambda i, j: (j, 0)),
    ]
    args = [x, gain.reshape(1, d), w1, w3, w2]
    if final:
        in_specs.append(pl.BlockSpec((1, d), lambda i, j: (0, 0)))
        args.append(final_gain.reshape(1, d))
    cast_specs = [pl.BlockSpec(*_cast_block(w.shape, t // tm, f // tf)) for w in casts]
    outs = pl.pallas_call(
        functools.partial(_ffn_kernel, final=final, ncast=len(casts)),
        out_shape=[jax.ShapeDtypeStruct((t, d), F32)]
        + [jax.ShapeDtypeStruct(w.shape, BF16) for w in casts],
        grid=(t // tm, f // tf),
        in_specs=in_specs + cast_specs,
        out_specs=[pl.BlockSpec((tm, d), lambda i, j: (i, 0))] + cast_specs,
        scratch_shapes=[pltpu.VMEM((tm, d), BF16)],
        compiler_params=_params(("arbitrary", "arbitrary")),
        name="ffn_final" if final else "ffn",
    )(*args, *casts)
    return outs


def _rotary(t, cos, sin, heads):
    dh = t.shape[1] // heads
    half = dh // 2
    out = []
    for h in range(heads):
        t1, t2 = t[:, h * dh:h * dh + half], t[:, h * dh + half:(h + 1) * dh]
        out += [t1 * cos - t2 * sin, t1 * sin + t2 * cos]
    return jnp.concatenate(out, axis=-1)


def _inproj_kernel(x_ref, g_ref, wa_ref, wb_ref, cos_ref, sin_ref, proj_ref, us5_ref, xn_ref,
                   *, heads):
    j = pl.program_id(1)
    tn = wa_ref.shape[1]

    @pl.when(j == 0)
    def _():
        xn = _rms(x_ref[...], g_ref[...]).astype(BF16)
        xn_ref[...] = xn
        res = _dot(xn, wa_ref[...])
        for k in range(us5_ref.shape[0]):
            us5_ref[k] = res[:, k * LANES:(k + 1) * LANES]

    @pl.when(j == 1)
    def _():
        xn = xn_ref[...]
        cos, sin = cos_ref[...], sin_ref[...]
        q = _rotary(_dot(xn, wa_ref[...]), cos, sin, heads)
        k = _rotary(_dot(xn, wb_ref[...]), cos, sin, heads) * ((tn // heads) ** -0.5)
        proj_ref[:, :tn] = q.astype(BF16)
        proj_ref[:, tn:] = k.astype(BF16)

    @pl.when(j > 1)
    def _():
        xn = xn_ref[...]
        proj_ref[:, :tn] = _dot(xn, wa_ref[...]).astype(BF16)
        proj_ref[:, tn:] = _dot(xn, wb_ref[...]).astype(BF16)


def _rope_tables(seq, dh):
    inv = ROPE_BASE ** (-jnp.arange(0, dh, 2, dtype=F32) / dh)
    ang = jnp.arange(seq, dtype=F32)[:, None] * inv[None, :]
    return jnp.cos(ang), jnp.sin(ang)


def _inproj(x, gain, w, s5_width, seq, *, tm=1024):
    t, d = x.shape
    n = w.shape[1]
    tn = s5_width
    tm = min(tm, seq)
    assert seq % tm == 0 and tn == d // 2, (seq, tm, tn, d)
    pairs, odd = divmod(n // tn - 1, 2)
    assert odd == 0, (n, tn)
    nt, nf = t // tm, 1 + pairs
    half = tn // RET_HEADS // 2
    cos, sin = _rope_tables(seq, 2 * half)
    pos = pl.BlockSpec((tm, half), lambda i, j: (i % (seq // tm), 0))
    return pl.pallas_call(
        functools.partial(_inproj_kernel, heads=RET_HEADS),
        out_shape=[jax.ShapeDtypeStruct((t, n - tn), BF16),
                   jax.ShapeDtypeStruct((tn // LANES, t, LANES), F32)],
        grid=(nt, nf),
        in_specs=[
            pl.BlockSpec((tm, d), _first_step_only(nt)),
            pl.BlockSpec((1, d), lambda i, j: (0, 0)),
            pl.BlockSpec((d, tn), lambda i, j: (0, jnp.maximum(2 * j - 1, 0))),
            pl.BlockSpec((d, tn), lambda i, j: (0, 2 * jnp.maximum(j, 1))),
            pos, pos,
        ],
        out_specs=[
            pl.BlockSpec((tm, 2 * tn), lambda i, j: (i, jnp.maximum(j - 1, 0))),
            pl.BlockSpec((tn // LANES, tm, LANES), lambda i, j: (0, i, 0)),
        ],
        scratch_shapes=[pltpu.VMEM((tm, d), BF16)],
        compiler_params=_params(("arbitrary", "arbitrary")),
        name="inproj",
    )(x, gain.reshape(1, d), w, w, cos, sin)


def _normmm_kernel(x_ref, g_ref, wk_ref, wv_ref, k_ref, v_ref, xn_ref):
    @pl.when(pl.program_id(0) == 0)
    def _():
        xn_ref[...] = _rms(x_ref[...], g_ref[...]).astype(BF16)

    xn = xn_ref[...]
    k_ref[...] = _dot(xn, wk_ref[...]).astype(BF16)
    v_ref[...] = _dot(xn, wv_ref[...]).astype(BF16)


def _normmm(x, gain, wk, wv, *, tn=1024):
    t, d = x.shape
    n = wk.shape[1]
    tn = min(tn, n)
    col = lambda rows: pl.BlockSpec((rows, tn), lambda j: (0, j))
    return pl.pallas_call(
        _normmm_kernel,
        out_shape=[jax.ShapeDtypeStruct((t, n), BF16)] * 2,
        grid=(n // tn,),
        in_specs=[_resident((t, d)), _resident((1, d)), col(d), col(d)],
        out_specs=[col(t), col(t)],
        scratch_shapes=[pltpu.VMEM((t, d), BF16)],
        compiler_params=_params(("arbitrary",)),
        name="normmm",
    )(x, gain.reshape(1, d), wk, wv)


def _s5_kernel(u_ref, m_ref, win_ref, wout_ref, lam_ref, d_ref, y_ref, z_ref, s_ref, *, tc, nb):
    rb = z_ref.shape[0] // nb
    seq = rb * tc
    u32 = jnp.concatenate(
        [jnp.concatenate([u_ref[pl.ds(b * seq + j, rb, stride=tc), :] for j in range(tc)], axis=-1)
         for b in range(nb)], axis=0)
    u = u32.astype(BF16)
    z_ref[...] = _dot(u, win_ref[0])
    ns = z_ref.shape[1] // 2
    lam_re = lam_ref[0, :, :ns]
    lam_im = lam_ref[0, :, ns:]

    per_tile = MXU_TILE // LANES
    blocks = [slice(cb * MXU_TILE, (cb + 1) * MXU_TILE) for cb in range(tc // per_tile)]
    y_intra = [_dot(u[:, :(cb + 1) * MXU_TILE], m_ref[0, :(cb + 1) * MXU_TILE, cols])
               + d_ref[0, :, cols] * u32[:, cols] for cb, cols in enumerate(blocks)]

    def cmul_add(a_re, a_im, x_re, x_im, y_re, y_im):
        return a_re * x_re - a_im * x_im + y_re, a_re * x_im + a_im * x_re + y_im

    stride = SCAN_STRIDE
    assert rb % stride == 0, (rb, stride)
    zero = jnp.zeros((1, ns), F32)
    lamk = (lam_re, lam_im)
    for _ in range(stride - 1):
        lamk = cmul_add(lam_re, lam_im, *lamk, zero, zero)
    for b in range(nb):
        s = (zero, zero)
        for c in range(0, rb, stride):
            rows = [slice(b * rb + c + i, b * rb + c + i + 1) for i in range(stride)]
            z = [(z_ref[r, :ns], z_ref[r, ns:]) for r in rows]
            t, w = s, z[0]
            for i, r in enumerate(rows):
                s_ref[r, :ns], s_ref[r, ns:] = t
                if i + 1 < stride:
                    t = cmul_add(lam_re, lam_im, *t, *z[i])
                    w = cmul_add(lam_re, lam_im, *w, *z[i + 1])
            s = cmul_add(*lamk, *s, *w)

    s = s_ref[...].astype(BF16)
    for cb, cols in enumerate(blocks):
        y = jax.nn.gelu(y_intra[cb] + _dot(s, wout_ref[0, :, cols]))
        for b in range(nb):
            for i in range(per_tile):
                y_ref[pl.ds(b * seq + cb * per_tile + i, rb, stride=tc), :] = (
                    y[b * rb:(b + 1) * rb, i * LANES:(i + 1) * LANES])


def _s5(u, m, w_in, w_out, lam, dtile, batch, *, nb=1):
    nlb, t, _ = u.shape
    seq = t // batch
    tc = S5_CHUNK
    _, width, nstate = w_in.shape
    nb = nb if batch % nb == 0 else 1
    rows = nb * seq // tc
    return pl.pallas_call(
        functools.partial(_s5_kernel, tc=tc, nb=nb),
        out_shape=jax.ShapeDtypeStruct((nlb, t, LANES), F32),
        grid=(nlb, batch // nb),
        in_specs=[
            pl.BlockSpec((None, nb * seq, LANES), lambda k, b: (k, b, 0)),
            pl.BlockSpec((1, width, width), lambda k, b: (k, 0, 0)),
            pl.BlockSpec((1, width, nstate), lambda k, b: (k, 0, 0)),
            pl.BlockSpec((1, nstate, width), lambda k, b: (k, 0, 0)),
            pl.BlockSpec((1, 1, nstate), lambda k, b: (k, 0, 0)),
            pl.BlockSpec((1, 1, width), lambda k, b: (k, 0, 0)),
        ],
        out_specs=pl.BlockSpec((None, nb * seq, LANES), lambda k, b: (k, b, 0)),
        scratch_shapes=[pltpu.VMEM((rows, nstate), F32), pltpu.VMEM((rows, nstate), F32)],
        compiler_params=_params(("parallel", "arbitrary")),
        name="s5",
    )(u, m, w_in, w_out, lam, dtile)


def _log2(n):
    assert n & (n - 1) == 0, n
    return n.bit_length() - 1


def _s5prep_kernel(bre_ref, bim_ref, cre_ref, cim_ref, lrow_ref, lcol_ref,
                   toep_ref, win_ref, wout_ref, lam_ref, *, tc):
    ns, gi = bre_ref.shape[1], bre_ref.shape[2]
    lanes, p = cre_ref.shape[1], cre_ref.shape[2]

    def iota(shape, dim):
        return lax.broadcasted_iota(jnp.int32, shape, dim)

    def grp(x, size):
        return lax.shift_right_logical(x, _log2(size))

    def split(a):
        hi = a.astype(BF16)
        return hi, (a - hi.astype(F32)).astype(BF16)

    def dot3(a, b):
        return _dot(a[0], b[0]) + _dot(a[0], b[1]) + _dot(a[1], b[0])

    e_i = ((iota((gi, lanes), 1) & (gi - 1)) == iota((gi, lanes), 0)).astype(BF16)
    e_p = ((iota((p, ns), 1) & (p - 1)) == iota((p, ns), 0)).astype(BF16)

    def repeat(a, e):
        hi, lo = split(a)
        return _dot(hi, e) + _dot(lo, e)

    mask_b = grp(iota((ns, lanes), 0), p) == grp(iota((ns, lanes), 1), gi)
    mask_c = grp(iota((lanes, ns), 0), gi) == grp(iota((lanes, ns), 1), p)

    def bdiag_b(ref):
        return jnp.where(mask_b, repeat(ref[0], e_i), 0.0).T

    def bdiag_c(ref):
        return jnp.where(mask_c, repeat(ref[0], e_p), 0.0).T

    b_re, b_im = bdiag_b(bre_ref), bdiag_b(bim_ref)
    c_re, c_im = bdiag_c(cre_ref), bdiag_c(cim_ref)
    c_re2, c_im2 = split(c_re), split(c_im)
    l_re, l_im = lrow_ref[0, 0:1, :], lrow_ref[0, 1:2, :]
    lc_re, lc_im = lcol_ref[0, :, 0:1], lcol_ref[0, :, 1:2]
    p_re, p_im = jnp.ones_like(l_re), jnp.zeros_like(l_im)
    q_re, q_im = jnp.ones_like(lc_re), jnp.zeros_like(lc_im)
    zero_tile = jnp.zeros((lanes, lanes), BF16)

    def tile(i):
        return slice(i * lanes, (i + 1) * lanes)

    for n in range(tc):
        lb_re = b_re * p_re - b_im * p_im
        lb_im = b_re * p_im + b_im * p_re
        win_ref[0, tile(tc - 1 - n), :ns] = lb_re.astype(BF16)
        win_ref[0, tile(tc - 1 - n), ns:] = lb_im.astype(BF16)
        kern = (dot3(split(lb_re), c_re2) - dot3(split(lb_im), c_im2)).astype(BF16)
        for j in range(tc - n):
            toep_ref[0, tile(j), tile(j + n)] = kern
        if n:
            for t in range(tc - n):
                toep_ref[0, tile(t + n), tile(t)] = zero_tile
        p_re, p_im = p_re * l_re - p_im * l_im, p_re * l_im + p_im * l_re
        q_re, q_im = q_re * lc_re - q_im * lc_im, q_re * lc_im + q_im * lc_re
        wout_ref[0, :ns, tile(n)] = (c_re * q_re - c_im * q_im).astype(BF16)
        wout_ref[0, ns:, tile(n)] = (-(c_re * q_im + c_im * q_re)).astype(BF16)
    lam_ref[0, :, :ns] = p_re
    lam_ref[0, :, ns:] = p_im


def _s5_weights(a_re, a_im, log_dt, b_re, b_im, c_re, c_im, d_skip):
    tc = S5_CHUNK
    g, p = a_re.shape
    gl = GROUPS_PER_LANE_BLOCK
    nlb = g // gl
    ns = gl * p
    dt = jnp.exp(log_dt)[:, None]
    mag = jnp.exp(a_re * dt)
    l_re = mag * jnp.cos(a_im * dt)
    l_im = mag * jnp.sin(a_im * dt)
    den = a_re * a_re + a_im * a_im
    n_re = l_re - 1.0
    n_im = l_im
    f_re = (n_re * a_re + n_im * a_im) / den
    f_im = (n_im * a_re - n_re * a_im) / den
    bb_re = (f_re[..., None] * b_re - f_im[..., None] * b_im).reshape(nlb, ns, S5_GROUP)
    bb_im = (f_re[..., None] * b_im + f_im[..., None] * b_re).reshape(nlb, ns, S5_GROUP)
    lrow = jnp.stack([l_re.reshape(nlb, ns), l_im.reshape(nlb, ns)], axis=1)
    lcol = jnp.stack([l_re.reshape(nlb, ns), l_im.reshape(nlb, ns)], axis=2)
    width = tc * LANES
    blk = lambda *shape: pl.BlockSpec((1,) + shape, lambda k: (k, 0, 0))
    toep, w_in, w_out, lam = pl.pallas_call(
        functools.partial(_s5prep_kernel, tc=tc),
        out_shape=(jax.ShapeDtypeStruct((nlb, width, width), BF16),
                   jax.ShapeDtypeStruct((nlb, width, 2 * ns), BF16),
                   jax.ShapeDtypeStruct((nlb, 2 * ns, width), BF16),
                   jax.ShapeDtypeStruct((nlb, 1, 2 * ns), F32)),
        grid=(nlb,),
        in_specs=[blk(ns, S5_GROUP), blk(ns, S5_GROUP), blk(LANES, p), blk(LANES, p),
                  blk(2, ns), blk(ns, 2)],
        out_specs=(blk(width, width), blk(width, 2 * ns), blk(2 * ns, width),
                   blk(1, 2 * ns)),
        compiler_params=_params(("parallel",)),
        name="s5prep",
    )(bb_re, bb_im, c_re.reshape(nlb, LANES, p), c_im.reshape(nlb, LANES, p), lrow, lcol)
    dtile = jnp.tile(d_skip.reshape(nlb, 1, LANES), (1, 1, tc))
    return toep, w_in, w_out, lam, dtile


def _ret_kernel(q_ref, k_ref, v_ref, dec_ref, xi_ref, zeta_ref, gam_ref, o_ref, state_ref,
                *, heads):
    c = pl.program_id(1)

    @pl.when(c == 0)
    def _():
        state_ref[...] = jnp.zeros_like(state_ref)

    dqk = q_ref.shape[2] // heads
    dv = v_ref.shape[2] // heads
    for b in range(q_ref.shape[0]):
        for h in range(heads):
            qb = q_ref[b, :, h * dqk:(h + 1) * dqk]
            kb = k_ref[b, :, h * dqk:(h + 1) * dqk]
            v = v_ref[b, :, h * dv:(h + 1) * dv]
            scores = _dot_nt(qb, kb) * dec_ref[h]
            st = state_ref[b, h]
            out = _dot(scores.astype(BF16), v) + _dot(qb, st.astype(BF16)) * xi_ref[h]
            kz = (kb.astype(F32) * zeta_ref[h]).astype(BF16)
            state_ref[b, h] = st * gam_ref[h] + _dot_tn(kz, v)
            o_ref[b, :, h * dv:(h + 1) * dv] = out.astype(BF16)


def _retention(proj, d, batch, seq, *, nb=4):
    t = proj.shape[0]
    heads = RET_HEADS
    nb = nb if batch % nb == 0 else 1
    proj3 = proj.reshape(batch, seq, proj.shape[1])
    ch = min(RET_KERNEL_CHUNK, seq)
    nc = seq // ch
    dqk = d // 2 // heads
    dv = d // heads
    log_gamma = jnp.log(1.0 - 2.0 ** (-5.0 - jnp.arange(heads, dtype=F32)))
    idx = jnp.arange(ch, dtype=F32)
    rel = idx[:, None] - idx[None, :]
    dec = jnp.where(rel[None] >= 0,
                    jnp.exp(jnp.maximum(rel, 0.0)[None] * log_gamma[:, None, None]), 0.0)
    xi = jnp.exp((idx + 1.0)[None, :] * log_gamma[:, None])[:, :, None]
    zeta = jnp.exp((ch - 1.0 - idx)[None, :] * log_gamma[:, None])[:, :, None]
    gam = jnp.exp(ch * log_gamma)[:, None, None]
    return pl.pallas_call(
        functools.partial(_ret_kernel, heads=heads),
        out_shape=jax.ShapeDtypeStruct((batch, seq, d), BF16),
        grid=(batch // nb, nc),
        in_specs=[
            pl.BlockSpec((nb, ch, d // 2), lambda b, c: (b, c, 0)),
            pl.BlockSpec((nb, ch, d // 2), lambda b, c: (b, c, 1)),
            pl.BlockSpec((nb, ch, d), lambda b, c: (b, c, 1)),
            pl.BlockSpec((heads, ch, ch), lambda b, c: (0, 0, 0)),
            pl.BlockSpec((heads, ch, 1), lambda b, c: (0, 0, 0)),
            pl.BlockSpec((heads, ch, 1), lambda b, c: (0, 0, 0)),
            pl.BlockSpec((heads, 1, 1), lambda b, c: (0, 0, 0)),
        ],
        out_specs=pl.BlockSpec((nb, ch, d), lambda b, c: (b, c, 0)),
        scratch_shapes=[pltpu.VMEM((nb, heads, dqk, dv), F32)],
        compiler_params=_params(("parallel", "arbitrary")),
        name="retention",
    )(proj3, proj3, proj3, dec, xi, zeta, gam).reshape(t, d)


def _mix_kernel(*refs, cn, ncast, heads):
    (ret_ref, g_ref, yg_ref, ga_ref, gb_ref, h_ref, wo_ref, wv_ref, wg_ref,
     wout_ref) = refs[:10]
    cast_in = refs[10:10 + ncast]
    o_ref = refs[10 + ncast]
    cast_out = refs[11 + ncast:]
    _run_casts(cast_in, cast_out)
    d = h_ref.shape[1]
    dv = d // heads
    blocks = [slice(j * cn, (j + 1) * cn) for j in range(d // cn)]
    yg = jnp.concatenate([yg_ref[k] for k in range(yg_ref.shape[0])], axis=-1).astype(BF16)
    y_a = [jax.nn.sigmoid(ga_ref[:, c].astype(F32)) * _dot(yg, wv_ref[:, c])
           * jax.nn.sigmoid(_dot(yg, wg_ref[:, c])) for c in blocks]
    og = []
    for hd in range(heads):
        cols = slice(hd * dv, (hd + 1) * dv)
        o = ret_ref[:, cols].astype(F32)
        cen = o - jnp.mean(o, axis=-1, keepdims=True)
        var = jnp.mean(cen * cen, axis=-1, keepdims=True)
        g = g_ref[:, cols].astype(F32)
        og.append((g * jax.nn.sigmoid(g) * (cen * lax.rsqrt(var + GN_EPS))).astype(BF16))
    og = jnp.concatenate(og, axis=-1)
    merged = [(y_a[j] + jax.nn.sigmoid(gb_ref[:, c].astype(F32)) * _dot(og, wo_ref[:, c]))
              .astype(BF16) for j, c in enumerate(blocks)]
    o_ref[...] = h_ref[...] + _dot(jnp.concatenate(merged, axis=-1), wout_ref[...])


def _mix(ret, yg, proj, h, w_o, w_v, w_g, w_out, casts=(), *, tm=256, cn=256):
    t, d = h.shape
    nlb = yg.shape[0]
    sw = nlb * LANES
    tm, cn = min(tm, t), min(cn, d)
    cast_specs = [pl.BlockSpec(*_cast_block(c.shape, t // tm, 1)) for c in casts]
    return pl.pallas_call(
        functools.partial(_mix_kernel, cn=cn, ncast=len(casts), heads=RET_HEADS),
        out_shape=[jax.ShapeDtypeStruct((t, d), F32)]
        + [jax.ShapeDtypeStruct(c.shape, BF16) for c in casts],
        grid=(t // tm, 1),
        in_specs=[
            pl.BlockSpec((tm, d), lambda i, j: (i, 0)),
            pl.BlockSpec((tm, d), lambda i, j: (i, 2)),
            pl.BlockSpec((nlb, tm, LANES), lambda i, j: (0, i, 0)),
            pl.BlockSpec((tm, d), lambda i, j: (i, 3)),
            pl.BlockSpec((tm, d), lambda i, j: (i, 4)),
            pl.BlockSpec((tm, d), lambda i, j: (i, 0)),
            _resident((d, d)), _resident((sw, d)), _resident((sw, d)), _resident((d, d)),
        ] + cast_specs,
        out_specs=[pl.BlockSpec((tm, d), lambda i, j: (i, 0))] + cast_specs,
        compiler_params=_params(("arbitrary", "arbitrary")),
        name="mix",
    )(ret, proj, yg, proj, proj, h, w_o, w_v, w_g, w_out, *casts)


def _xattn_kernel(h_ref, g_ref, wq_ref, k_ref, v_ref, wo_ref, o_ref, *, heads):
    h = h_ref[...]
    d = h.shape[1]
    dh = d // heads
    xn = _rms(h, g_ref[...]).astype(BF16)
    outs = []
    for hd in range(heads):
        cols = slice(hd * dh, (hd + 1) * dh)
        q = _dot(xn, wq_ref[:, cols])
        s = _dot_nt(q.astype(BF16), k_ref[0, :, cols]) * (dh ** -0.5)
        e = jnp.exp(s - jnp.max(s, axis=-1, keepdims=True))
        p = e / jnp.sum(e, axis=-1, keepdims=True)
        outs.append(_dot(p.astype(BF16), v_ref[0, :, cols]).astype(BF16))
    o_ref[...] = h + _dot(jnp.concatenate(outs, axis=-1), wo_ref[...])


def _xattn(h, gain, wq, kmem, vmem, wo, batch, *, tm=1024):
    t, d = h.shape
    seq = t // batch
    mlen = kmem.shape[1]
    tm = min(tm, seq)
    nt = seq // tm
    return pl.pallas_call(
        functools.partial(_xattn_kernel, heads=XATTN_HEADS),
        out_shape=jax.ShapeDtypeStruct((t, d), F32),
        grid=(batch, nt),
        in_specs=[
            pl.BlockSpec((tm, d), lambda b, i: (b * nt + i, 0)),
            _resident((1, d)),
            _resident((d, d)),
            pl.BlockSpec((1, mlen, d), lambda b, i: (b, 0, 0)),
            pl.BlockSpec((1, mlen, d), lambda b, i: (b, 0, 0)),
            _resident((d, d)),
        ],
        out_specs=pl.BlockSpec((tm, d), lambda b, i: (b * nt + i, 0)),
        compiler_params=_params(("arbitrary", "arbitrary")),
        name="xattn",
    )(h, gain.reshape(1, d), wq, kmem, vmem, wo)


def kernel(x, mem, ffn1_norm, ffn1_w1, ffn1_w3, ffn1_w2, mix_norm, w_in, s5_a_re, s5_a_im, s5_log_dt, s5_b_re, s5_b_im, s5_c_re, s5_c_im, s5_d, s5_glu_v, s5_glu_g, ret_w_o, w_out, xattn_norm, mem_norm, xattn_wq, xattn_wk, xattn_wv, xattn_wo, ffn2_norm, ffn2_w1, ffn2_w3, ffn2_w2, final_norm):
    batch, seq, d = x.shape
    mlen = mem.shape[1]
    depth = ffn1_w1.shape[0]
    t = batch * seq
    s5_width = s5_d.shape[1]
    bf = lambda w: w.astype(BF16)

    h = x.reshape(t, d)
    mem2 = mem.reshape(batch * mlen, d)
    for l in range(depth):
        last = l == depth - 1
        later = (w_in, s5_glu_v, s5_glu_g, ret_w_o, w_out, xattn_wq, xattn_wk, xattn_wv, xattn_wo)
        h, c_in, c_gv, c_gg, c_ro, c_out, c_wq, c_wk, c_wv, c_wo = _ffn(
            h, ffn1_norm[l], bf(ffn1_w1[l]), bf(ffn1_w3[l]), bf(ffn1_w2[l]),
            casts=[w[l] for w in later])
        proj, us5 = _inproj(h, mix_norm[l], c_in, s5_width, seq)
        toep, s_in, s_out, lam, dtile = _s5_weights(
            s5_a_re[l], s5_a_im[l], s5_log_dt[l], s5_b_re[l], s5_b_im[l],
            s5_c_re[l], s5_c_im[l], s5_d[l])
        yg = _s5(us5, toep, s_in, s_out, lam, dtile, batch)
        ret = _retention(proj, d, batch, seq)
        h, c_f1, c_f3, c_f2 = _mix(ret, yg, proj, h, c_ro, c_gv, c_gg, c_out,
                                   casts=[ffn2_w1[l], ffn2_w3[l], ffn2_w2[l]])
        kmem, vmem = _normmm(mem2, mem_norm[l], c_wk, c_wv)
        h = _xattn(h, xattn_norm[l], c_wq, kmem.reshape(batch, mlen, d),
                   vmem.reshape(batch, mlen, d), c_wo, batch)
        h, = _ffn(h, ffn2_norm[l], c_f1, c_f3, c_f2, final_norm if last else None)
    if depth == 0:
        raise ValueError("depth must be >= 1")
    return h.reshape(batch, seq, d)
```
